```python
import jax, jax.numpy as jnp
from jax import lax
import numpy as np

D_MODEL = 1024
BATCH = 8
SEQ = 2048
DEPTH = 2
DEC_BATCH = 128
DEC_SEQ = 8
PAST_LEN = 16384
PAGE_SIZE = 128

SSD_HEADS = 16
SSD_HEAD_DIM = 64
SSD_INNER = SSD_HEADS * SSD_HEAD_DIM
SSD_GROUPS = 4
SSD_STATE = 128
SSD_CONV = 4
SSD_CHUNK = 128
SSD_CONV_DIM = SSD_INNER + 2 * SSD_GROUPS * SSD_STATE
CF_DIM = D_MODEL
CF_CONV = 31
PLE_DIM = 256
PEER_HEADS = 8
PEER_KEYS = 128
PEER_EXPERTS = PEER_KEYS * PEER_KEYS
PEER_QDIM = 256
PEER_HALF = PEER_QDIM // 2
PEER_TOPK = 16
PEER_BLOCK = 256
COL_Z = SSD_INNER
COL_XBC = COL_Z + SSD_CONV_DIM
COL_DT = COL_XBC + SSD_HEADS
COL_GLU = COL_DT + 2 * CF_DIM
IN_COLS = COL_GLU + 2 * D_MODEL
EPS = 1e-6

kernel_name = 'ssd_conformer_peer_hybrid_step'


def rmsnorm(x, g):
    xf = x.astype(jnp.float32)
    y = xf * lax.rsqrt(jnp.mean(xf * xf, axis=-1, keepdims=True) + EPS)
    return (y * g.astype(jnp.float32)).astype(x.dtype)


def layernorm(x, g, b):
    xf = x.astype(jnp.float32)
    mu = jnp.mean(xf, axis=-1, keepdims=True)
    xc = xf - mu
    y = xc * lax.rsqrt(jnp.mean(xc * xc, axis=-1, keepdims=True) + EPS)
    return (y * g.astype(jnp.float32) + b.astype(jnp.float32)).astype(x.dtype)


def causal_dwconv(buf, u, w, b):
    width = w.shape[0]
    xp = jnp.concatenate([buf.astype(u.dtype), u], axis=1)
    out = lax.conv_general_dilated(xp, w.astype(u.dtype)[:, None, :], window_strides=(1,),
                                   padding='VALID', dimension_numbers=('NWC', 'WIO', 'NWC'),
                                   feature_group_count=u.shape[-1])
    return out + b.astype(u.dtype), xp[:, xp.shape[1] - (width - 1):]


def segsum(a):
    T = a.shape[-1]
    rep = jnp.broadcast_to(a[..., :, None], a.shape + (T,))
    strict = jnp.tril(jnp.ones((T, T), dtype=bool), -1)
    cs = jnp.cumsum(jnp.where(strict, rep, 0.0), axis=-2)
    return jnp.where(jnp.tril(jnp.ones((T, T), dtype=bool)), cs, -jnp.inf)


def ssd_scan(xh, dt, A, Bm, Cm, h0):
    f32 = jnp.float32
    b, l = xh.shape[0], xh.shape[1]
    cl = min(SSD_CHUNK, l)
    pad = (-l) % cl
    if pad:
        padl = lambda t: jnp.pad(t, [(0, 0), (0, pad)] + [(0, 0)] * (t.ndim - 2))
        xh, dt, Bm, Cm = padl(xh), padl(dt), padl(Bm), padl(Cm)
    nc = (l + pad) // cl
    G, R, P, N = SSD_GROUPS, SSD_HEADS // SSD_GROUPS, SSD_HEAD_DIM, SSD_STATE
    X = (xh.astype(f32) * dt[..., None]).reshape(b, nc, cl, G, R, P)
    dA = (dt * A).reshape(b, nc, cl, G, R).transpose(0, 3, 4, 1, 2)
    Bc = Bm.astype(f32).reshape(b, nc, cl, G, N)
    Cc = Cm.astype(f32).reshape(b, nc, cl, G, N)
    A_cs = jnp.cumsum(dA, axis=-1)
    Lm = jnp.exp(segsum(dA))
    CB = jnp.einsum('bclgn,bcsgn->bcgls', Cc, Bc)
    y_diag = jnp.einsum('bcgls,bgrcls,bcsgrp->bclgrp', CB, Lm, X)
    decay = jnp.exp(A_cs[..., -1:] - A_cs)
    st = jnp.einsum('bcsgn,bgrcs,bcsgrp->bcgrpn', Bc, decay, X)
    st = jnp.concatenate([h0.astype(f32).reshape(b, 1, G, R, P, N), st], axis=1)
    last = jnp.pad(A_cs[..., -1], [(0, 0), (0, 0), (0, 0), (1, 0)])
    chunk_decay = jnp.exp(segsum(last))
    st = jnp.einsum('bgrzc,bcgrpn->bzgrpn', chunk_decay, st)
    y_off = jnp.einsum('bclgn,bcgrpn,bgrcl->bclgrp', Cc, st[:, :-1], jnp.exp(A_cs))
    y = (y_diag + y_off).reshape(b, nc * cl, SSD_HEADS, P)[:, :l]
    return y, st[:, -1].reshape(b, SSD_HEADS, P, N)


def ssd_branch(z, xbc_raw, dt_raw, conv_buf, h0, conv_w, conv_b, dt_bias, a_log, d_skip, norm_g):
    b, l = z.shape[0], z.shape[1]
    xbc, new_buf = causal_dwconv(conv_buf, xbc_raw, conv_w, conv_b)
    xbc = jax.nn.silu(xbc)
    xs = xbc[..., :SSD_INNER].reshape(b, l, SSD_HEADS, SSD_HEAD_DIM)
    Bm = xbc[..., SSD_INNER:SSD_INNER + SSD_GROUPS * SSD_STATE].reshape(b, l, SSD_GROUPS, SSD_STATE)
    Cm = xbc[..., SSD_INNER + SSD_GROUPS * SSD_STATE:].reshape(b, l, SSD_GROUPS, SSD_STATE)
    dt = jax.nn.softplus(dt_raw.astype(jnp.float32) + dt_bias.astype(jnp.float32))
    A = -jnp.exp(a_log.astype(jnp.float32))
    y, hT = ssd_scan(xs, dt, A, Bm, Cm, h0)
    y = y + d_skip.astype(jnp.float32)[:, None] * xs.astype(jnp.float32)
    yz = y.reshape(b, l, SSD_INNER) * jax.nn.silu(z.astype(jnp.float32))
    yg = yz.reshape(b, l, SSD_GROUPS, SSD_INNER // SSD_GROUPS)
    yg = yg * lax.rsqrt(jnp.mean(yg * yg, axis=-1, keepdims=True) + EPS)
    y = yg.reshape(b, l, SSD_INNER) * norm_g.astype(jnp.float32)
    return y.astype(z.dtype), new_buf, hT.astype(h0.dtype)


def conformer_branch(glu_in, conv_buf, dw_w, dw_b, ln_g, ln_b):
    a, g = glu_in[..., :CF_DIM], glu_in[..., CF_DIM:]
    u = a * jax.nn.sigmoid(g)
    c, new_buf = causal_dwconv(conv_buf, u, dw_w, dw_b)
    return jax.nn.silu(layernorm(c, ln_g, ln_b)), new_buf


def peer(x, w_q, sub_keys, u_tab, v_tab):
    b, l, d = x.shape
    T = b * l
    xt = x.reshape(T, d)
    pad = (-T) % PEER_BLOCK
    xt = jnp.pad(xt, [(0, pad), (0, 0)])

    def block(xb):
        q = (xb @ w_q).reshape(xb.shape[0], PEER_HEADS, 2, PEER_HALF)
        s = jnp.einsum('nhjd,hjkd->nhjk', q, sub_keys).astype(jnp.float32)
        sv, si = lax.top_k(s, PEER_TOPK)
        cand = (sv[:, :, 0, :, None] + sv[:, :, 1, None, :]).reshape(xb.shape[0], PEER_HEADS, -1)
        cidx = (si[:, :, 0, :, None] * PEER_KEYS + si[:, :, 1, None, :]).reshape(xb.shape[0], PEER_HEADS, -1)
        top_s, pos = lax.top_k(cand, PEER_TOPK)
        eidx = jnp.take_along_axis(cidx, pos, axis=-1)
        gate = jax.nn.softmax(top_s, axis=-1)
        pre = jnp.einsum('nd,nhkd->nhk', xb, u_tab[eidx]).astype(jnp.float32)
        act = (jax.nn.gelu(pre) * gate).astype(xb.dtype)
        return jnp.einsum('nhk,nhkd->nd', act, v_tab[eidx])

    out = lax.map(block, xt.reshape(-1, PEER_BLOCK, d)).reshape(-1, d)[:T]
    return out.reshape(b, l, d)


def run_trunk(x, p, ssd_conv0, ssd_h0, cf_conv0, W):
    h = x
    new_h, new_sconv, new_cfconv = [], [], []
    for i in range(DEPTH):
        n = rmsnorm(h, W['g_mix'][i])
        proj = n @ W['w_in'][i]
        z = proj[..., :COL_Z]
        xbc = proj[..., COL_Z:COL_XBC]
        dt_raw = proj[..., COL_XBC:COL_DT]
        glu = proj[..., COL_DT:COL_GLU]
        g_a, g_b = proj[..., COL_GLU:COL_GLU + D_MODEL], proj[..., COL_GLU + D_MODEL:]
        y_a, sconv, hT = ssd_branch(z, xbc, dt_raw, ssd_conv0[i], ssd_h0[i], W['ssd_conv_w'][i],
                                    W['ssd_conv_b'][i], W['ssd_dt_bias'][i], W['ssd_a_log'][i],
                                    W['ssd_d'][i], W['ssd_norm_g'][i])
        y_b, cfconv = conformer_branch(glu, cf_conv0[i], W['cf_dw_w'][i], W['cf_dw_b'][i],
                                       W['cf_ln_g'][i], W['cf_ln_b'][i])
        mix = (jax.nn.sigmoid(g_a) * (y_a @ W['w_ssd_out'][i])
               + jax.nn.sigmoid(g_b) * (y_b @ W['w_cf_out'][i]))
        h = h + mix @ W['w_o'][i]
        h = h + peer(rmsnorm(h, W['g_ffn'][i]), W['peer_wq'][i], W['peer_keys'][i],
                     W['peer_u'][i], W['peer_v'][i])
        gate = jax.nn.sigmoid(rmsnorm(h, W['g_ple'][i]) @ W['w_ple_gate'][i])
        h = h + gate * (p[i] @ W['w_ple_proj'][i])
        new_h.append(hT)
        new_sconv.append(sconv)
        new_cfconv.append(cfconv)
    y = rmsnorm(h, W['g_final'])
    return y, jnp.stack(new_h), jnp.stack(new_sconv), jnp.stack(new_cfconv)


def setup_inputs(seed: int = 0) -> dict:
    key = jax.random.key(seed)
    ks = iter(jax.random.split(key, 48))
    f32 = jnp.float32

    def nrm(shape, scale):
        return jax.random.normal(next(ks), shape, f32) * scale

    def uni(shape, lo, hi):
        return jax.random.uniform(next(ks), shape, f32, lo, hi)

    dt0 = jnp.exp(uni((DEPTH, SSD_HEADS), float(np.log(1e-3)), float(np.log(1e-1))))
    return {
        'x_prompt': nrm((BATCH, SEQ, D_MODEL), 1.0),
        'x_sample': nrm((DEC_BATCH, DEC_SEQ, D_MODEL), 1.0),
        'state_ssd': nrm((DEPTH, DEC_BATCH, SSD_HEADS, SSD_HEAD_DIM, SSD_STATE), 0.1),
        'state_ssd_conv': nrm((DEPTH, DEC_BATCH, SSD_CONV - 1, SSD_CONV_DIM), 1.0),
        'state_cf_conv': nrm((DEPTH, DEC_BATCH, CF_CONV - 1, CF_DIM), 0.5),
        'p_prompt': nrm((DEPTH, BATCH, SEQ, PLE_DIM), 1.0),
        'p_sample': nrm((DEPTH, DEC_BATCH, DEC_SEQ, PLE_DIM), 1.0),
        'g_mix': 1.0 + nrm((DEPTH, D_MODEL), 0.02),
        'w_in': nrm((DEPTH, D_MODEL, IN_COLS), D_MODEL ** -0.5),
        'ssd_conv_w': nrm((DEPTH, SSD_CONV, SSD_CONV_DIM), SSD_CONV ** -0.5),
        'ssd_conv_b': nrm((DEPTH, SSD_CONV_DIM), 0.02),
        'ssd_dt_bias': dt0 + jnp.log(-jnp.expm1(-dt0)),
        'ssd_a_log': jnp.log(uni((DEPTH, SSD_HEADS), 1.0, 16.0)),
        'ssd_d': 1.0 + nrm((DEPTH, SSD_HEADS), 0.02),
        'ssd_norm_g': 1.0 + nrm((DEPTH, SSD_INNER), 0.02),
        'w_ssd_out': nrm((DEPTH, SSD_INNER, D_MODEL), SSD_INNER ** -0.5),
        'cf_dw_w': nrm((DEPTH, CF_CONV, CF_DIM), CF_CONV ** -0.5),
        'cf_dw_b': nrm((DEPTH, CF_DIM), 0.02),
        'cf_ln_g': 1.0 + nrm((DEPTH, CF_DIM), 0.02),
        'cf_ln_b': nrm((DEPTH, CF_DIM), 0.02),
        'w_cf_out': nrm((DEPTH, CF_DIM, D_MODEL), CF_DIM ** -0.5),
        'w_o': nrm((DEPTH, D_MODEL, D_MODEL), 0.5 * D_MODEL ** -0.5),
        'g_ffn': 1.0 + nrm((DEPTH, D_MODEL), 0.02),
        'peer_wq': nrm((DEPTH, D_MODEL, PEER_HEADS * PEER_QDIM), D_MODEL ** -0.5),
        'peer_keys': nrm((DEPTH, PEER_HEADS, 2, PEER_KEYS, PEER_HALF), PEER_HALF ** -0.5),
        'peer_u': nrm((DEPTH, PEER_EXPERTS, D_MODEL), D_MODEL ** -0.5),
        'peer_v': nrm((DEPTH, PEER_EXPERTS, D_MODEL), 0.1),
        'g_ple': 1.0 + nrm((DEPTH, D_MODEL), 0.02),
        'w_ple_gate': nrm((DEPTH, D_MODEL, D_MODEL), D_MODEL ** -0.5),
        'w_ple_proj': nrm((DEPTH, PLE_DIM, D_MODEL), 0.5 * PLE_DIM ** -0.5),
        'g_final': 1.0 + nrm((D_MODEL,), 0.02),
    }


def reference(x_prompt, x_sample, state_ssd, state_ssd_conv, state_cf_conv, p_prompt, p_sample,
              g_mix, w_in, ssd_conv_w, ssd_conv_b, ssd_dt_bias, ssd_a_log, ssd_d, ssd_norm_g,
              w_ssd_out, cf_dw_w, cf_dw_b, cf_ln_g, cf_ln_b, w_cf_out, w_o, g_ffn,
              peer_wq, peer_keys, peer_u, peer_v, g_ple, w_ple_gate, w_ple_proj, g_final):
    W = dict(g_mix=g_mix, w_in=w_in, ssd_conv_w=ssd_conv_w, ssd_conv_b=ssd_conv_b,
             ssd_dt_bias=ssd_dt_bias, ssd_a_log=ssd_a_log, ssd_d=ssd_d, ssd_norm_g=ssd_norm_g,
             w_ssd_out=w_ssd_out, cf_dw_w=cf_dw_w, cf_dw_b=cf_dw_b, cf_ln_g=cf_ln_g,
             cf_ln_b=cf_ln_b, w_cf_out=w_cf_out, w_o=w_o, g_ffn=g_ffn, peer_wq=peer_wq,
             peer_keys=peer_keys, peer_u=peer_u, peer_v=peer_v, g_ple=g_ple,
             w_ple_gate=w_ple_gate, w_ple_proj=w_ple_proj, g_final=g_final)
    bp, dty = x_prompt.shape[0], x_prompt.dtype
    h0_p = jnp.zeros((DEPTH, bp, SSD_HEADS, SSD_HEAD_DIM, SSD_STATE), dty)
    sconv0_p = jnp.zeros((DEPTH, bp, SSD_CONV - 1, SSD_CONV_DIM), dty)
    cfconv0_p = jnp.zeros((DEPTH, bp, CF_CONV - 1, CF_DIM), dty)
    y_prompt, ssd_p, sconv_p, cfconv_p = run_trunk(x_prompt, p_prompt, sconv0_p, h0_p, cfconv0_p, W)
    y_sample, ssd_s, sconv_s, cfconv_s = run_trunk(x_sample, p_sample, state_ssd_conv, state_ssd,
                                                   state_cf_conv, W)
    return (y_prompt, y_sample, ssd_p, sconv_p, cfconv_p, ssd_s, sconv_s, cfconv_s)
```

```python
import functools
import math

import jax
import jax.numpy as jnp
from jax import lax
from jax.experimental import pallas as pl
from jax.experimental.pallas import tpu as pltpu

F32, BF16 = jnp.float32, jnp.bfloat16
EPS = 1e-6
HIGHEST = lax.Precision.HIGHEST

SSD_HEADS, SSD_HEAD_DIM, SSD_GROUPS, SSD_STATE, SSD_CONV, SSD_CHUNK = 16, 64, 4, 128, 4, 128
CF_CONV = 31
PEER_HEADS, PEER_KEYS, PEER_TOPK = 8, 128, 16
LANES = 128
BF16_ROWS = 16
VMEM_LIMIT = 56 * 1024 * 1024

CAND_PAIRS = [(r1, r2) for r1 in range(PEER_TOPK) for r2 in range(PEER_TOPK)
              if (r1 + 1) * (r2 + 1) <= PEER_TOPK]


def _nt(a, b, **kw):
    return lax.dot_general(a, b, (((1,), (1,)), ((), ())), preferred_element_type=F32, **kw)


def _tn(a, b, **kw):
    return lax.dot_general(a, b, (((0,), (0,)), ((), ())), preferred_element_type=F32, **kw)


def _mm(a, b, **kw):
    return jnp.dot(a, b, preferred_element_type=F32, **kw)


def _rms(x, g):
    return x * lax.rsqrt(jnp.mean(x * x, axis=-1, keepdims=True) + EPS) * g


def _silu(x):
    return x * jax.nn.sigmoid(x)


def _params(sem):
    return pltpu.CompilerParams(dimension_semantics=sem, vmem_limit_bytes=VMEM_LIMIT)


def _inproj_kernel(x_ref, g_ref, w_ref, o_ref, xn_ref):
    @pl.when(pl.program_id(1) == 0)
    def _():
        xn_ref[...] = _rms(x_ref[...], g_ref[...]).astype(BF16)

    o_ref[...] = _mm(xn_ref[...], w_ref[...])


def _inproj(h, g, w, tm, tn):
    T, D = h.shape
    N = w.shape[1]
    return pl.pallas_call(
        _inproj_kernel,
        grid=(T // tm, N // tn),
        in_specs=[pl.BlockSpec((tm, D), lambda i, j: (i, 0)),
                  pl.BlockSpec((1, D), lambda i, j: (0, 0)),
                  pl.BlockSpec((D, tn), lambda i, j: (0, j))],
        out_specs=pl.BlockSpec((tm, tn), lambda i, j: (i, j)),
        out_shape=jax.ShapeDtypeStruct((T, N), F32),
        scratch_shapes=[pltpu.VMEM((tm, D), BF16)],
        compiler_params=_params(("parallel", "arbitrary")),
        name="inproj",
    )(h, g, w)


def _ssd_kernel(z_ref, xs_ref, bc_ref, dt_ref, conv0_ref, h0_ref, cw_ref, cb_ref, dtb_ref, alog_ref,
                dexp_ref, ng_ref, e_ref, et_ref,
                y_ref, nconv_ref, ht_ref,
                xpad_ref, st_ref, yscr_ref, *, L):
    c = pl.program_id(1)
    inner = SSD_HEADS * SSD_HEAD_DIM
    hist = SSD_CONV - 1
    base = 8

    @pl.when(c == 0)
    def _():
        xpad_ref[base - hist:base, :] = conv0_ref[0]
        st_ref[...] = h0_ref[0]

    xpad_ref[base:base + L, 0:inner] = xs_ref[...]
    xpad_ref[base:base + L, inner:] = bc_ref[...]

    conv = cb_ref[...] + cw_ref[0:1, :] * xpad_ref[base - hist:base - hist + L, :]
    for k in range(1, SSD_CONV):
        conv = conv + cw_ref[k:k + 1, :] * xpad_ref[base - hist + k:base - hist + k + L, :]
    last_rows = xpad_ref[base + L - hist:base + L, :]
    nconv_ref[0] = last_rows
    xpad_ref[base - hist:base, :] = last_rows
    xc = _silu(conv)
    xs = xc[:, :inner]
    gn = SSD_GROUPS * SSD_STATE
    bm = xc[:, inner:inner + gn].astype(BF16)
    cm = xc[:, inner + gn:].astype(BF16)

    x = dt_ref[...] + dtb_ref[...]
    dt = jnp.maximum(x, 0.0) + jnp.log1p(jnp.exp(-jnp.abs(x)))
    da = dt * (-jnp.exp(alog_ref[...]))
    row = lax.broadcasted_iota(jnp.int32, (L, L), 0)
    col = lax.broadcasted_iota(jnp.int32, (L, L), 1)
    causal = row >= col
    acs = _mm(causal.astype(F32), da, precision=HIGHEST)
    e = e_ref[...]
    dt_x = _mm(dt, e, precision=HIGHEST)
    acs_x = _mm(acs, e, precision=HIGHEST)
    acs_t = _nt(et_ref[...], acs, precision=HIGHEST)
    alast_x = acs_x[L - 1:L, :]
    xd = xs * dt_x
    xd_bf = xd.astype(BF16)
    xdec = (xd * jnp.exp(alast_x - acs_x)).astype(BF16)
    eacs = jnp.exp(acs_x)
    lane = lax.broadcasted_iota(jnp.int32, (L, 2 * SSD_HEAD_DIM), 1)

    heads_per_group = SSD_HEADS // SSD_GROUPS
    for g in range(SSD_GROUPS):
        bg = bm[:, g * SSD_STATE:(g + 1) * SSD_STATE]
        cg = cm[:, g * SSD_STATE:(g + 1) * SSD_STATE]
        cb = _nt(cg, bg)
        for pp in range(heads_per_group // 2):
            p = g * (heads_per_group // 2) + pp
            cols = slice(p * 2 * SSD_HEAD_DIM, (p + 1) * 2 * SSD_HEAD_DIM)
            xp = xd_bf[:, cols]
            ys = []
            for hh in range(2):
                h = 2 * p + hh
                acol = acs[:, h:h + 1]
                arow = acs_t[h * SSD_HEAD_DIM:h * SSD_HEAD_DIM + 1, :]
                lm = jnp.where(causal, jnp.exp(acol - arow), 0.0)
                ys.append(_mm((cb * lm).astype(BF16), xp))
            ydiag = jnp.where(lane < SSD_HEAD_DIM, ys[0], ys[1])
            st = st_ref[p]
            yoff = _nt(cg, st.astype(BF16)) * eacs[:, cols]
            yscr_ref[:, cols] = ydiag + yoff
            scale = jnp.exp(acs_t[p * 2 * SSD_HEAD_DIM:(p + 1) * 2 * SSD_HEAD_DIM, L - 1:L])
            st_ref[p] = st * scale + _tn(xdec[:, cols], bg)

    ht_ref[0] = st_ref[...]

    y = (yscr_ref[...] + dexp_ref[...] * xs) * _silu(z_ref[...])
    gw = inner // SSD_GROUPS
    for g in range(SSD_GROUPS):
        seg = y[:, g * gw:(g + 1) * gw]
        seg = seg * lax.rsqrt(jnp.mean(seg * seg, axis=-1, keepdims=True) + EPS)
        y_ref[:, g * gw:(g + 1) * gw] = seg * ng_ref[:, g * gw:(g + 1) * gw]


def _ssd(proj, conv0, h0, wts, *, nb, nc, L, row_off):
    inner = SSD_HEADS * SSD_HEAD_DIM
    cd = conv0.shape[-1]
    r0 = row_off // L
    rows = lambda b, c: r0 + b * nc + c
    dt_blk = (proj.shape[1] - LANES) // LANES
    const = lambda shape: pl.BlockSpec(shape, lambda b, c: (0,) * len(shape))
    npair = SSD_HEADS // 2
    y, nconv, ht = pl.pallas_call(
        functools.partial(_ssd_kernel, L=L),
        grid=(nb, nc),
        in_specs=[pl.BlockSpec((L, inner), lambda b, c: (rows(b, c), 0)),
                  pl.BlockSpec((L, inner), lambda b, c: (rows(b, c), 1)),
                  pl.BlockSpec((L, inner), lambda b, c: (rows(b, c), 2)),
                  pl.BlockSpec((L, LANES), lambda b, c: (rows(b, c), dt_blk)),
                  pl.BlockSpec((1, SSD_CONV - 1, cd), lambda b, c: (b, 0, 0)),
                  pl.BlockSpec((1, npair, LANES, SSD_STATE), lambda b, c: (b, 0, 0, 0)),
                  const((SSD_CONV, cd)), const((1, cd)), const((1, LANES)), const((1, LANES)),
                  const((1, inner)), const((1, inner)), const((LANES, inner)), const((inner, LANES))],
        out_specs=[pl.BlockSpec((L, inner), lambda b, c: (b * nc + c, 0)),
                   pl.BlockSpec((1, SSD_CONV - 1, cd), lambda b, c: (b, 0, 0)),
                   pl.BlockSpec((1, npair, LANES, SSD_STATE), lambda b, c: (b, 0, 0, 0))],
        out_shape=[jax.ShapeDtypeStruct((nb * nc * L, inner), F32),
                   jax.ShapeDtypeStruct((nb, SSD_CONV - 1, cd), F32),
                   jax.ShapeDtypeStruct((nb, npair, LANES, SSD_STATE), F32)],
        scratch_shapes=[pltpu.VMEM((8 + L, cd), F32),
                        pltpu.VMEM((npair, LANES, SSD_STATE), F32),
                        pltpu.VMEM((L, inner), F32)],
        compiler_params=_params(("parallel", "arbitrary")),
        name=f"ssd_L{L}",
    )(proj, proj, proj, proj, conv0, h0.reshape(nb, npair, LANES, SSD_STATE), *wts)
    return y, nconv, ht.reshape(nb, SSD_HEADS, SSD_HEAD_DIM, SSD_STATE)


def _cf_kernel(a_ref, g_ref, buf0_ref, w_ref, b_ref, lng_ref, lnb_ref, y_ref, nbuf_ref,
               upad_ref, conv_ref, *, L):
    c = pl.program_id(1)
    hist = CF_CONV - 1
    base = 32
    dim = a_ref.shape[1]

    @pl.when(c == 0)
    def _():
        upad_ref[base - hist:base, :] = buf0_ref[0]

    upad_ref[base:base + L, :] = a_ref[...] * jax.nn.sigmoid(g_ref[...])
    slab = 2 * LANES
    for s in range(dim // slab):
        cols = slice(s * slab, (s + 1) * slab)
        acc = b_ref[:, cols] + w_ref[0:1, cols] * upad_ref[base - hist:base - hist + L, cols]
        for k in range(1, CF_CONV):
            acc = acc + w_ref[k:k + 1, cols] * upad_ref[base - hist + k:base - hist + k + L, cols]
        conv_ref[:, cols] = acc
    nbuf = upad_ref[base + L - hist:base + L, :]
    nbuf_ref[0] = nbuf
    upad_ref[base - hist:base, :] = nbuf

    x = conv_ref[...]
    xc = x - jnp.mean(x, axis=-1, keepdims=True)
    y = xc * lax.rsqrt(jnp.mean(xc * xc, axis=-1, keepdims=True) + EPS) * lng_ref[...] + lnb_ref[...]
    y_ref[...] = _silu(y)


def _conformer(proj, buf0, wts, *, nb, nc, L, row_off):
    dim = buf0.shape[-1]
    r0 = row_off // L
    rows = lambda b, c: r0 + b * nc + c
    const = lambda shape: pl.BlockSpec(shape, lambda b, c: (0,) * len(shape))
    return pl.pallas_call(
        functools.partial(_cf_kernel, L=L),
        grid=(nb, nc),
        in_specs=[pl.BlockSpec((L, dim), lambda b, c: (rows(b, c), 3)),
                  pl.BlockSpec((L, dim), lambda b, c: (rows(b, c), 4)),
                  pl.BlockSpec((1, CF_CONV - 1, dim), lambda b, c: (b, 0, 0)),
                  const((CF_CONV, dim)), const((1, dim)), const((1, dim)), const((1, dim))],
        out_specs=[pl.BlockSpec((L, dim), lambda b, c: (b * nc + c, 0)),
                   pl.BlockSpec((1, CF_CONV - 1, dim), lambda b, c: (b, 0, 0))],
        out_shape=[jax.ShapeDtypeStruct((nb * nc * L, dim), F32),
                   jax.ShapeDtypeStruct((nb, CF_CONV - 1, dim), F32)],
        scratch_shapes=[pltpu.VMEM((32 + L, dim), F32), pltpu.VMEM((L, dim), F32)],
        compiler_params=_params(("parallel", "arbitrary")),
        name=f"conformer_L{L}",
    )(proj, proj, buf0, *wts)


def _merge_kernel(h_ref, ya_ref, yb_ref, ga_ref, gb_ref, wa_ref, wb_ref, wo_ref, o_ref):
    mix = (jax.nn.sigmoid(ga_ref[...]) * _mm(ya_ref[...].astype(BF16), wa_ref[...])
           + jax.nn.sigmoid(gb_ref[...]) * _mm(yb_ref[...].astype(BF16), wb_ref[...]))
    o_ref[...] = h_ref[...] + _mm(mix.astype(BF16), wo_ref[...])


def _merge(h, ya, yb, proj, wa, wb, wo, tm):
    T, D = h.shape
    rowblk = lambda j: pl.BlockSpec((tm, D), lambda i: (i, j))
    wspec = pl.BlockSpec((D, D), lambda i: (0, 0))
    return pl.pallas_call(
        _merge_kernel,
        grid=(T // tm,),
        in_specs=[rowblk(0), rowblk(0), rowblk(0), rowblk(5), rowblk(6), wspec, wspec, wspec],
        out_specs=rowblk(0),
        out_shape=jax.ShapeDtypeStruct((T, D), F32),
        compiler_params=_params(("parallel",)),
        name="merge",
    )(h, ya, yb, proj, proj, wa, wb, wo)


def _extract_topk(s_ref, rank_ref, vals_ref):
    rank_ref[...] = jnp.full(rank_ref.shape, float(PEER_TOPK), F32)
    n = s_ref.shape[0]

    def step(k, carry):
        s = s_ref[...]
        m = jnp.max(s, axis=0)
        idx = lax.broadcasted_iota(jnp.int32, s.shape, 0)
        first = jnp.min(jnp.where(s == m[None], idx, n), axis=0)
        hit = idx == first[None]
        rank_ref[...] = jnp.where(hit, k.astype(F32), rank_ref[...])
        s_ref[...] = jnp.where(hit, -jnp.inf, s)
        vals_ref[k] = m
        return carry

    lax.fori_loop(0, PEER_TOPK, step, 0)


def _select_kernel(h_ref, g_ref, wq_ref, kb1_ref, kb2_ref,
                   xnt_ref, n_ref, e1_ref, r2_ref, e2_ref,
                   s1_scr, s2_scr, w_scr, rk1_scr, rk2_scr, v1_scr, v2_scr,
                   cw_scr, crk_scr, cv_scr, pk_scr, *, tb):
    H, K = PEER_HEADS, PEER_KEYS
    half = H * K
    xn = _rms(h_ref[...], g_ref[...])
    xnt_ref[...] = xn.T.astype(BF16)
    q = _mm(xn.astype(BF16), wq_ref[...]).astype(BF16)
    s1 = _nt(kb1_ref[...], q[:, :half])
    s2 = _nt(kb2_ref[...], q[:, half:])
    nlt = tb // LANES
    for lt in range(nlt):
        s1_scr[lt] = s1[:, lt * LANES:(lt + 1) * LANES].reshape(K, H, LANES)
        s2_scr[lt] = s2[:, lt * LANES:(lt + 1) * LANES].reshape(K, H, LANES)

    def lane_tile(lt, carry):
        w_scr[...] = s1_scr[lt]
        _extract_topk(w_scr, rk1_scr, v1_scr)
        w_scr[...] = s2_scr[lt]
        _extract_topk(w_scr, rk2_scr, v2_scr)
        for p, (r1, r2) in enumerate(CAND_PAIRS):
            cw_scr[p] = v1_scr[r1] + v2_scr[r2]
        _extract_topk(cw_scr, crk_scr, cv_scr)
        top = cv_scr[0]
        zsum = jnp.zeros_like(top)
        for k in range(PEER_TOPK):
            zsum = zsum + jnp.exp(cv_scr[k] - top)
        rz = 1.0 / zsum
        rk1 = rk1_scr[...]
        n_i = jnp.zeros_like(rk1)
        for r1 in range(PEER_TOPK):
            cnt = jnp.zeros_like(top)
            for p, (a, _) in enumerate(CAND_PAIRS):
                if a == r1:
                    cnt = cnt + jnp.where(crk_scr[p] < float(PEER_TOPK), 1.0, 0.0)
            n_i = jnp.where(rk1 == float(r1), cnt[None], n_i)
        pk_scr[0] = n_i.reshape(half, LANES)
        pk_scr[1] = (jnp.exp(s1_scr[lt] - v1_scr[0][None]) * rz[None]).reshape(half, LANES)
        pk_scr[2] = rk2_scr[...].reshape(half, LANES)
        pk_scr[3] = jnp.exp(s2_scr[lt] - v2_scr[0][None]).reshape(half, LANES)
        lanes = pl.ds(pl.multiple_of(lt * LANES, LANES), LANES)
        for h in range(H):
            n_ref[h, :, lanes] = pk_scr[0, pl.ds(h, K, stride=H), :]
            e1_ref[h, :, lanes] = pk_scr[1, pl.ds(h, K, stride=H), :]
            r2_ref[h, :, lanes] = pk_scr[2, pl.ds(h, K, stride=H), :].astype(BF16)
            e2_ref[h, :, lanes] = pk_scr[3, pl.ds(h, K, stride=H), :].astype(BF16)
        return carry

    lax.fori_loop(0, nlt, lane_tile, 0)


def _peer_select(h1, g, wq, kb1, kb2, tb):
    T, D = h1.shape
    H, K = PEER_HEADS, PEER_KEYS
    nlt = tb // LANES
    ncand = len(CAND_PAIRS)
    const = lambda shape: pl.BlockSpec(shape, lambda i: (0,) * len(shape))
    hkt = lambda dt: jax.ShapeDtypeStruct((H, K, T), dt)
    hkt_spec = pl.BlockSpec((H, K, tb), lambda i: (0, 0, i))
    tile = lambda n: pltpu.VMEM((n, H, LANES), F32)
    return pl.pallas_call(
        functools.partial(_select_kernel, tb=tb),
        grid=(T // tb,),
        in_specs=[pl.BlockSpec((tb, D), lambda i: (i, 0)), const((1, D)), const(wq.shape),
                  const(kb1.shape), const(kb2.shape)],
        out_specs=[pl.BlockSpec((D, tb), lambda i: (0, i)), hkt_spec, hkt_spec, hkt_spec, hkt_spec],
        out_shape=[jax.ShapeDtypeStruct((D, T), BF16), hkt(F32), hkt(F32), hkt(BF16), hkt(BF16)],
        scratch_shapes=[pltpu.VMEM((nlt, K, H, LANES), F32), pltpu.VMEM((nlt, K, H, LANES), F32),
                        tile(K), tile(K), tile(K), tile(PEER_TOPK), tile(PEER_TOPK),
                        tile(ncand), tile(ncand), tile(PEER_TOPK),
                        pltpu.VMEM((4, H * K, LANES), F32)],
        compiler_params=_params(("parallel",)),
        name="peer_select",
    )(h1, g, wq, kb1, kb2)


def _gelu_tanh(x):
    c = math.sqrt(2.0 / math.pi)
    return x * (0.5 * (1.0 + jnp.tanh(c * (x + 0.044715 * (x * x * x)))))


def _dense_kernel(h_ref, xnt_ref, n_ref, e1_ref, r2_ref, e2_ref, u_ref, vt_ref, o_ref,
                  pre_scr, act_scr, acc_scr, *, tb, ec):
    c = pl.program_id(1)
    H, K = PEER_HEADS, PEER_KEYS

    @pl.when(c == 0)
    def _():
        acc_scr[...] = jnp.zeros_like(acc_scr)

    pre_scr[...] = _mm(u_ref[...], xnt_ref[...])
    ni = ec // K
    zero = jnp.zeros((), BF16)

    def per_lane_tile(lt, carry):
        lanes = pl.ds(pl.multiple_of(lt * LANES, LANES), LANES)
        for i8 in range(ni // 8):
            irows = pl.ds(pl.multiple_of(c * ni + i8 * 8, 8), 8)
            n8 = [n_ref[h, irows, lanes] for h in range(H)]
            e8 = [e1_ref[h, irows, lanes] for h in range(H)]
            for i1 in range(8):
                il = i8 * 8 + i1
                nb = [jnp.broadcast_to(n8[h][i1:i1 + 1, :], (BF16_ROWS, LANES)).astype(BF16) for h in range(H)]
                eb = [jnp.broadcast_to(e8[h][i1:i1 + 1, :], (BF16_ROWS, LANES)).astype(BF16) for h in range(H)]
                for jt in range(K // BF16_ROWS):
                    js = slice(jt * BF16_ROWS, (jt + 1) * BF16_ROWS)
                    gate = jnp.where(r2_ref[0, js, lanes] < nb[0], e2_ref[0, js, lanes], zero) * eb[0]
                    for h in range(1, H):
                        gate = gate + jnp.where(r2_ref[h, js, lanes] < nb[h], e2_ref[h, js, lanes], zero) * eb[h]
                    rows = slice(il * K + jt * BF16_ROWS, il * K + (jt + 1) * BF16_ROWS)
                    act = _gelu_tanh(pre_scr[rows, lanes]) * gate.astype(F32)
                    act_scr[rows, lanes] = act.astype(BF16)
        return carry

    lax.fori_loop(0, tb // LANES, per_lane_tile, 0)

    acc_scr[...] += _mm(vt_ref[...], act_scr[...])

    @pl.when(c == pl.num_programs(1) - 1)
    def _():
        o_ref[...] = h_ref[...] + acc_scr[...].T


def _peer_dense(h1, xnt, n, e1, r2, e2, u, vt, tb, ec):
    T, D = h1.shape
    H, K = PEER_HEADS, PEER_KEYS
    E = u.shape[0]
    hkt_spec = pl.BlockSpec((H, K, tb), lambda i, c: (0, 0, i))
    return pl.pallas_call(
        functools.partial(_dense_kernel, tb=tb, ec=ec),
        grid=(T // tb, E // ec),
        in_specs=[pl.BlockSpec((tb, D), lambda i, c: (i, 0)),
                  pl.BlockSpec((D, tb), lambda i, c: (0, i)),
                  hkt_spec, hkt_spec, hkt_spec, hkt_spec,
                  pl.BlockSpec((ec, D), lambda i, c: (c, 0)),
                  pl.BlockSpec((D, ec), lambda i, c: (0, c))],
        out_specs=pl.BlockSpec((tb, D), lambda i, c: (i, 0)),
        out_shape=jax.ShapeDtypeStruct((T, D), F32),
        scratch_shapes=[pltpu.VMEM((ec, tb), F32), pltpu.VMEM((ec, tb), BF16), pltpu.VMEM((D, tb), F32)],
        compiler_params=_params(("parallel", "arbitrary")),
        name="peer_dense",
    )(h1, xnt, n, e1, r2, e2, u, vt)


def _ple_kernel(h_ref, p_ref, g_ref, wg_ref, wp_ref, gf_ref, o_ref, *, final):
    h = h_ref[...]
    gate = jax.nn.sigmoid(_mm(_rms(h, g_ref[...]).astype(BF16), wg_ref[...]))
    h = h + gate * _mm(p_ref[...].astype(BF16), wp_ref[...])
    o_ref[...] = _rms(h, gf_ref[...]) if final else h


def _ple(h, p, g, wg, wp, gf, tm, final):
    T, D = h.shape
    P = p.shape[1]
    const = lambda shape: pl.BlockSpec(shape, lambda i: (0,) * len(shape))
    return pl.pallas_call(
        functools.partial(_ple_kernel, final=final),
        grid=(T // tm,),
        in_specs=[pl.BlockSpec((tm, D), lambda i: (i, 0)), pl.BlockSpec((tm, P), lambda i: (i, 0)),
                  const((1, D)), const((D, D)), const((P, D)), const((1, D))],
        out_specs=pl.BlockSpec((tm, D), lambda i: (i, 0)),
        out_shape=jax.ShapeDtypeStruct((T, D), F32),
        compiler_params=_params(("parallel",)),
        name="ple",
    )(h, p, g, wg, wp, gf)


def _tiles(T):
    tm = next(t for t in (512, 256, 128, 64, 32, 16, 8) if T % t == 0)
    tb = next(t for t in (512, 256, 128) if T % t == 0)
    return tm, tb


def _pad_lanes(v, fill=0.0):
    return jnp.pad(v.astype(F32), (0, LANES - v.shape[0]), constant_values=fill)[None, :]


def kernel(x_prompt, x_sample, state_ssd, state_ssd_conv, state_cf_conv, p_prompt, p_sample, g_mix, w_in, ssd_conv_w, ssd_conv_b, ssd_dt_bias, ssd_a_log, ssd_d, ssd_norm_g, w_ssd_out, cf_dw_w, cf_dw_b, cf_ln_g, cf_ln_b, w_cf_out, w_o, g_ffn, peer_wq, peer_keys, peer_u, peer_v, g_ple, w_ple_gate, w_ple_proj, g_final):
    depth = w_in.shape[0]
    bp, lp, D = x_prompt.shape
    bs, ls, _ = x_sample.shape
    tp, ts = bp * lp, bs * ls
    T = tp + ts
    inner = SSD_HEADS * SSD_HEAD_DIM
    cd = state_ssd_conv.shape[-1]
    H, K = PEER_HEADS, PEER_KEYS
    tm, tb = _tiles(T)
    lc_p, lc_s = min(SSD_CHUNK, lp), min(SSD_CHUNK, ls)
    assert lp % lc_p == 0 and ls % lc_s == 0 and tp % lc_s == 0 and D == inner == cf_dw_w.shape[-1]

    h = jnp.concatenate([x_prompt.reshape(tp, D), x_sample.reshape(ts, D)], axis=0)
    expand = (jnp.arange(inner)[None, :] // SSD_HEAD_DIM == jnp.arange(LANES)[:, None]).astype(F32)
    zeros_conv = jnp.zeros((bp, SSD_CONV - 1, cd), F32)
    zeros_h = jnp.zeros((bp, SSD_HEADS, SSD_HEAD_DIM, SSD_STATE), F32)
    zeros_cf = jnp.zeros((bp, CF_CONV - 1, D), F32)
    row = lambda v: v.astype(F32)[None, :]
    eye_h = jnp.eye(H, dtype=F32)
    outs = {k: [] for k in ("ssd_p", "sconv_p", "cf_p", "ssd_s", "sconv_s", "cf_s")}

    for i in range(depth):
        w = w_in[i]
        o_xbc, o_dt = inner, inner + cd
        o_glu = o_dt + SSD_HEADS
        w_r = jnp.concatenate([w[:, :o_xbc], w[:, o_xbc:o_dt], w[:, o_glu:],
                               jnp.pad(w[:, o_dt:o_glu], ((0, 0), (0, LANES - SSD_HEADS)))], axis=1).astype(BF16)
        ncols = w_r.shape[1]
        tn = next(t for t in (ncols // 3, ncols) if t % LANES == 0 and ncols % t == 0)
        proj = _inproj(h, row(g_mix[i]), w_r, tm, tn)

        ssd_w = (ssd_conv_w[i], row(ssd_conv_b[i]), _pad_lanes(ssd_dt_bias[i]), _pad_lanes(ssd_a_log[i]),
                 row(jnp.repeat(ssd_d[i], SSD_HEAD_DIM)), row(ssd_norm_g[i]), expand, expand.T)
        ya_p, sconv_p, ssd_p = _ssd(proj, zeros_conv, zeros_h, ssd_w, nb=bp, nc=lp // lc_p, L=lc_p, row_off=0)
        ya_s, sconv_s, ssd_s = _ssd(proj, state_ssd_conv[i], state_ssd[i], ssd_w,
                                    nb=bs, nc=ls // lc_s, L=lc_s, row_off=tp)
        cf_w = (cf_dw_w[i], row(cf_dw_b[i]), row(cf_ln_g[i]), row(cf_ln_b[i]))
        yb_p, cf_p = _conformer(proj, zeros_cf, cf_w, nb=bp, nc=lp // lc_p, L=lc_p, row_off=0)
        yb_s, cf_s = _conformer(proj, state_cf_conv[i], cf_w, nb=bs, nc=ls // lc_s, L=lc_s, row_off=tp)
        for k, v in (("ssd_p", ssd_p), ("sconv_p", sconv_p), ("cf_p", cf_p),
                     ("ssd_s", ssd_s), ("sconv_s", sconv_s), ("cf_s", cf_s)):
            outs[k].append(v)
        ya = jnp.concatenate([ya_p, ya_s], axis=0)
        yb = jnp.concatenate([yb_p, yb_s], axis=0)
        h = _merge(h, ya, yb, proj, w_ssd_out[i].astype(BF16), w_cf_out[i].astype(BF16), w_o[i].astype(BF16), tm)

        qd = peer_keys.shape[-1]
        wq = peer_wq[i].reshape(D, H, 2, qd).transpose(0, 2, 1, 3).reshape(D, 2 * H * qd).astype(BF16)
        kb = [jnp.einsum("hkd,hg->khgd", peer_keys[i, :, s], eye_h).reshape(K * H, H * qd).astype(BF16)
              for s in range(2)]
        xnt, n_sel, e1, r2, e2 = _peer_select(h, row(g_ffn[i]), wq, kb[0], kb[1], tb)
        h = _peer_dense(h, xnt, n_sel, e1, r2, e2, peer_u[i].astype(BF16), peer_v[i].T.astype(BF16),
                        tb, ec=8 * K)

        p_all = jnp.concatenate([p_prompt[i].reshape(tp, -1), p_sample[i].reshape(ts, -1)], axis=0)
        h = _ple(h, p_all, row(g_ple[i]), w_ple_gate[i].astype(BF16), w_ple_proj[i].astype(BF16),
                 row(g_final), tm, final=(i == depth - 1))

    y_prompt = h[:tp].reshape(bp, lp, D)
    y_sample = h[tp:].reshape(bs, ls, D)
    st = {k: jnp.stack(v) for k, v in outs.items()}
    return (y_prompt, y_sample, st["ssd_p"], st["sconv_p"], st["cf_p"], st["ssd_s"], st["sconv_s"], st["cf_s"])
```

```python
import functools
import math

import jax
import jax.numpy as jnp
from jax import lax
from jax.experimental import pallas as pl
from jax.experimental.pallas import tpu as pltpu

F32, BF16 = jnp.float32, jnp.bfloat16
PACKED = jnp.uint32
EPS = 1e-6
HIGHEST = lax.Precision.HIGHEST

SSD_HEADS, SSD_HEAD_DIM, SSD_GROUPS, SSD_STATE, SSD_CONV, SSD_CHUNK = 16, 64, 4, 128, 4, 128
CF_CONV = 31
PEER_HEADS, PEER_KEYS, PEER_TOPK = 8, 128, 16
LANES = 128
BF16_ROWS = 16
VMEM_LIMIT = 56 * 1024 * 1024

CAND_PAIRS = [(r1, r2) for r1 in range(PEER_TOPK) for r2 in range(PEER_TOPK)
              if (r1 + 1) * (r2 + 1) <= PEER_TOPK]


def _nt(a, b, **kw):
    return lax.dot_general(a, b, (((1,), (1,)), ((), ())), preferred_element_type=F32, **kw)


def _tn(a, b, **kw):
    return lax.dot_general(a, b, (((0,), (0,)), ((), ())), preferred_element_type=F32, **kw)


def _mm(a, b, **kw):
    return jnp.dot(a, b, preferred_element_type=F32, **kw)


def _rms(x, g):
    return x * lax.rsqrt(jnp.mean(x * x, axis=-1, keepdims=True) + EPS) * g


def _silu(x):
    return x * jax.nn.sigmoid(x)


def _pack_pair(lo, hi):
    bits = lambda v: lax.bitcast_convert_type(v.astype(BF16).astype(F32), PACKED)
    return (bits(hi) & jnp.uint32(0xFFFF0000)) | lax.shift_right_logical(bits(lo), jnp.uint32(16))


def _unpack_pair(packed, half):
    bits = (packed & jnp.uint32(0xFFFF0000)) if half else lax.shift_left(packed, jnp.uint32(16))
    return lax.bitcast_convert_type(bits, F32)


def _params(sem, flags=None):
    return pltpu.CompilerParams(dimension_semantics=sem, vmem_limit_bytes=VMEM_LIMIT, flags=flags)


def _inproj_kernel(x_ref, g_ref, w_ref, o_ref, xn_ref):
    @pl.when(pl.program_id(1) == 0)
    def _():
        xn_ref[...] = _rms(x_ref[...], g_ref[...]).astype(BF16)

    o_ref[...] = _mm(xn_ref[...], w_ref[...])


def _inproj(h, g, w, tm, tn):
    T, D = h.shape
    N = w.shape[1]
    return pl.pallas_call(
        _inproj_kernel,
        grid=(T // tm, N // tn),
        in_specs=[pl.BlockSpec((tm, D), lambda i, j: (i, 0)),
                  pl.BlockSpec((1, D), lambda i, j: (0, 0)),
                  pl.BlockSpec((D, tn), lambda i, j: (0, j))],
        out_specs=pl.BlockSpec((tm, tn), lambda i, j: (i, j)),
        out_shape=jax.ShapeDtypeStruct((T, N), F32),
        scratch_shapes=[pltpu.VMEM((tm, D), BF16)],
        compiler_params=_params(("parallel", "arbitrary")),
        name="inproj",
    )(h, g, w)


def _ssd_kernel(z_ref, xs_ref, bc_ref, dt_ref, conv0_ref, h0_ref, cw_ref, cb_ref, dtb_ref, alog_ref,
                dexp_ref, ng_ref, e_ref, et_ref,
                y_ref, nconv_ref, ht_ref,
                xpad_ref, st_ref, yscr_ref, *, L):
    c = pl.program_id(1)
    inner = SSD_HEADS * SSD_HEAD_DIM
    hist = SSD_CONV - 1
    base = 8

    @pl.when(c == 0)
    def _():
        xpad_ref[base - hist:base, :] = conv0_ref[0]
        st_ref[...] = h0_ref[0]

    xpad_ref[base:base + L, 0:inner] = xs_ref[...]
    xpad_ref[base:base + L, inner:] = bc_ref[...]

    conv = cb_ref[...] + cw_ref[0:1, :] * xpad_ref[base - hist:base - hist + L, :]
    for k in range(1, SSD_CONV):
        conv = conv + cw_ref[k:k + 1, :] * xpad_ref[base - hist + k:base - hist + k + L, :]
    last_rows = xpad_ref[base + L - hist:base + L, :]
    nconv_ref[0] = last_rows
    xpad_ref[base - hist:base, :] = last_rows
    xc = _silu(conv)
    xs = xc[:, :inner]
    gn = SSD_GROUPS * SSD_STATE
    bm = xc[:, inner:inner + gn].astype(BF16)
    cm = xc[:, inner + gn:].astype(BF16)

    x = dt_ref[...] + dtb_ref[...]
    dt = jnp.maximum(x, 0.0) + jnp.log1p(jnp.exp(-jnp.abs(x)))
    da = dt * (-jnp.exp(alog_ref[...]))
    row = lax.broadcasted_iota(jnp.int32, (L, L), 0)
    col = lax.broadcasted_iota(jnp.int32, (L, L), 1)
    causal = row >= col
    acs = _mm(causal.astype(F32), da, precision=HIGHEST)
    e = e_ref[...]
    dt_x = _mm(dt, e, precision=HIGHEST)
    acs_x = _mm(acs, e, precision=HIGHEST)
    acs_t = _nt(et_ref[...], acs, precision=HIGHEST)
    alast_x = acs_x[L - 1:L, :]
    xd = xs * dt_x
    xd_bf = xd.astype(BF16)
    xdec = (xd * jnp.exp(alast_x - acs_x)).astype(BF16)
    eacs = jnp.exp(acs_x)
    lane = lax.broadcasted_iota(jnp.int32, (L, 2 * SSD_HEAD_DIM), 1)

    heads_per_group = SSD_HEADS // SSD_GROUPS
    for g in range(SSD_GROUPS):
        bg = bm[:, g * SSD_STATE:(g + 1) * SSD_STATE]
        cg = cm[:, g * SSD_STATE:(g + 1) * SSD_STATE]
        cb = _nt(cg, bg)
        for pp in range(heads_per_group // 2):
            p = g * (heads_per_group // 2) + pp
            cols = slice(p * 2 * SSD_HEAD_DIM, (p + 1) * 2 * SSD_HEAD_DIM)
            xp = xd_bf[:, cols]
            ys = []
            for hh in range(2):
                h = 2 * p + hh
                acol = acs[:, h:h + 1]
                arow = acs_t[h * SSD_HEAD_DIM:h * SSD_HEAD_DIM + 1, :]
                lm = jnp.where(causal, jnp.exp(acol - arow), 0.0)
                ys.append(_mm((cb * lm).astype(BF16), xp))
            ydiag = jnp.where(lane < SSD_HEAD_DIM, ys[0], ys[1])
            st = st_ref[p]
            yoff = _nt(cg, st.astype(BF16)) * eacs[:, cols]
            yscr_ref[:, cols] = ydiag + yoff
            scale = jnp.exp(acs_t[p * 2 * SSD_HEAD_DIM:(p + 1) * 2 * SSD_HEAD_DIM, L - 1:L])
            st_ref[p] = st * scale + _tn(xdec[:, cols], bg)

    ht_ref[0] = st_ref[...]

    y = (yscr_ref[...] + dexp_ref[...] * xs) * _silu(z_ref[...])
    gw = inner // SSD_GROUPS
    for g in range(SSD_GROUPS):
        seg = y[:, g * gw:(g + 1) * gw]
        seg = seg * lax.rsqrt(jnp.mean(seg * seg, axis=-1, keepdims=True) + EPS)
        y_ref[:, g * gw:(g + 1) * gw] = seg * ng_ref[:, g * gw:(g + 1) * gw]


def _ssd(proj, conv0, h0, wts, *, nb, nc, L, row_off):
    inner = SSD_HEADS * SSD_HEAD_DIM
    cd = conv0.shape[-1]
    r0 = row_off // L
    rows = lambda b, c: r0 + b * nc + c
    dt_blk = (proj.shape[1] - LANES) // LANES
    const = lambda shape: pl.BlockSpec(shape, lambda b, c: (0,) * len(shape))
    npair = SSD_HEADS // 2
    y, nconv, ht = pl.pallas_call(
        functools.partial(_ssd_kernel, L=L),
        grid=(nb, nc),
        in_specs=[pl.BlockSpec((L, inner), lambda b, c: (rows(b, c), 0)),
                  pl.BlockSpec((L, inner), lambda b, c: (rows(b, c), 1)),
                  pl.BlockSpec((L, inner), lambda b, c: (rows(b, c), 2)),
                  pl.BlockSpec((L, LANES), lambda b, c: (rows(b, c), dt_blk)),
                  pl.BlockSpec((1, SSD_CONV - 1, cd), lambda b, c: (b, 0, 0)),
                  pl.BlockSpec((1, npair, LANES, SSD_STATE), lambda b, c: (b, 0, 0, 0)),
                  const((SSD_CONV, cd)), const((1, cd)), const((1, LANES)), const((1, LANES)),
                  const((1, inner)), const((1, inner)), const((LANES, inner)), const((inner, LANES))],
        out_specs=[pl.BlockSpec((L, inner), lambda b, c: (b * nc + c, 0)),
                   pl.BlockSpec((1, SSD_CONV - 1, cd), lambda b, c: (b, 0, 0)),
                   pl.BlockSpec((1, npair, LANES, SSD_STATE), lambda b, c: (b, 0, 0, 0))],
        out_shape=[jax.ShapeDtypeStruct((nb * nc * L, inner), F32),
                   jax.ShapeDtypeStruct((nb, SSD_CONV - 1, cd), F32),
                   jax.ShapeDtypeStruct((nb, npair, LANES, SSD_STATE), F32)],
        scratch_shapes=[pltpu.VMEM((8 + L, cd), F32),
                        pltpu.VMEM((npair, LANES, SSD_STATE), F32),
                        pltpu.VMEM((L, inner), F32)],
        compiler_params=_params(("parallel", "arbitrary")),
        name=f"ssd_L{L}",
    )(proj, proj, proj, proj, conv0, h0.reshape(nb, npair, LANES, SSD_STATE), *wts)
    return y, nconv, ht.reshape(nb, SSD_HEADS, SSD_HEAD_DIM, SSD_STATE)


def _cf_kernel(a_ref, g_ref, buf0_ref, w_ref, b_ref, lng_ref, lnb_ref, y_ref, nbuf_ref,
               upad_ref, conv_ref, *, L):
    c = pl.program_id(1)
    hist = CF_CONV - 1
    base = 32
    dim = a_ref.shape[1]

    @pl.when(c == 0)
    def _():
        upad_ref[base - hist:base, :] = buf0_ref[0]

    upad_ref[base:base + L, :] = a_ref[...] * jax.nn.sigmoid(g_ref[...])
    slab = 2 * LANES
    for s in range(dim // slab):
        cols = slice(s * slab, (s + 1) * slab)
        acc = b_ref[:, cols] + w_ref[0:1, cols] * upad_ref[base - hist:base - hist + L, cols]
        for k in range(1, CF_CONV):
            acc = acc + w_ref[k:k + 1, cols] * upad_ref[base - hist + k:base - hist + k + L, cols]
        conv_ref[:, cols] = acc
    nbuf = upad_ref[base + L - hist:base + L, :]
    nbuf_ref[0] = nbuf
    upad_ref[base - hist:base, :] = nbuf

    x = conv_ref[...]
    xc = x - jnp.mean(x, axis=-1, keepdims=True)
    y = xc * lax.rsqrt(jnp.mean(xc * xc, axis=-1, keepdims=True) + EPS) * lng_ref[...] + lnb_ref[...]
    y_ref[...] = _silu(y)


def _conformer(proj, buf0, wts, *, nb, nc, L, row_off):
    dim = buf0.shape[-1]
    r0 = row_off // L
    rows = lambda b, c: r0 + b * nc + c
    const = lambda shape: pl.BlockSpec(shape, lambda b, c: (0,) * len(shape))
    return pl.pallas_call(
        functools.partial(_cf_kernel, L=L),
        grid=(nb, nc),
        in_specs=[pl.BlockSpec((L, dim), lambda b, c: (rows(b, c), 3)),
                  pl.BlockSpec((L, dim), lambda b, c: (rows(b, c), 4)),
                  pl.BlockSpec((1, CF_CONV - 1, dim), lambda b, c: (b, 0, 0)),
                  const((CF_CONV, dim)), const((1, dim)), const((1, dim)), const((1, dim))],
        out_specs=[pl.BlockSpec((L, dim), lambda b, c: (b * nc + c, 0)),
                   pl.BlockSpec((1, CF_CONV - 1, dim), lambda b, c: (b, 0, 0))],
        out_shape=[jax.ShapeDtypeStruct((nb * nc * L, dim), F32),
                   jax.ShapeDtypeStruct((nb, CF_CONV - 1, dim), F32)],
        scratch_shapes=[pltpu.VMEM((32 + L, dim), F32), pltpu.VMEM((L, dim), F32)],
        compiler_params=_params(("parallel", "arbitrary")),
        name=f"conformer_L{L}",
    )(proj, proj, buf0, *wts)


def _merge_kernel(h_ref, ya_ref, yb_ref, ga_ref, gb_ref, wa_ref, wb_ref, wo_ref, o_ref):
    mix = (jax.nn.sigmoid(ga_ref[...]) * _mm(ya_ref[...].astype(BF16), wa_ref[...])
           + jax.nn.sigmoid(gb_ref[...]) * _mm(yb_ref[...].astype(BF16), wb_ref[...]))
    o_ref[...] = h_ref[...] + _mm(mix.astype(BF16), wo_ref[...])


def _merge(h, ya, yb, proj, wa, wb, wo, tm):
    T, D = h.shape
    rowblk = lambda j: pl.BlockSpec((tm, D), lambda i: (i, j))
    wspec = pl.BlockSpec((D, D), lambda i: (0, 0))
    return pl.pallas_call(
        _merge_kernel,
        grid=(T // tm,),
        in_specs=[rowblk(0), rowblk(0), rowblk(0), rowblk(5), rowblk(6), wspec, wspec, wspec],
        out_specs=rowblk(0),
        out_shape=jax.ShapeDtypeStruct((T, D), F32),
        compiler_params=_params(("parallel",)),
        name="merge",
    )(h, ya, yb, proj, proj, wa, wb, wo)


def _extract_topk(s_ref, rank_ref, vals_ref):
    rank_ref[...] = jnp.full(rank_ref.shape, float(PEER_TOPK), F32)
    n = s_ref.shape[0]

    def step(k, carry):
        s = s_ref[...]
        m = jnp.max(s, axis=0)
        idx = lax.broadcasted_iota(jnp.int32, s.shape, 0)
        first = jnp.min(jnp.where(s == m[None], idx, n), axis=0)
        hit = idx == first[None]
        rank_ref[...] = jnp.where(hit, k.astype(F32), rank_ref[...])
        s_ref[...] = jnp.where(hit, -jnp.inf, s)
        vals_ref[k] = m
        return carry

    lax.fori_loop(0, PEER_TOPK, step, 0)


def _select_kernel(h_ref, g_ref, wq_ref, kb1_ref, kb2_ref,
                   xnt_ref, n_ref, e1_ref, r2_ref, e2_ref,
                   s1_scr, s2_scr, w_scr, rk1_scr, rk2_scr, v1_scr, v2_scr,
                   cw_scr, crk_scr, cv_scr, pk_scr, *, tb):
    H, K = PEER_HEADS, PEER_KEYS
    half = H * K
    xn = _rms(h_ref[...], g_ref[...])
    xnt_ref[...] = xn.T.astype(BF16)
    q = _mm(xn.astype(BF16), wq_ref[...]).astype(BF16)
    s1 = _nt(kb1_ref[...], q[:, :half])
    s2 = _nt(kb2_ref[...], q[:, half:])
    nlt = tb // LANES
    for lt in range(nlt):
        s1_scr[lt] = s1[:, lt * LANES:(lt + 1) * LANES].reshape(K, H, LANES)
        s2_scr[lt] = s2[:, lt * LANES:(lt + 1) * LANES].reshape(K, H, LANES)

    def lane_tile(lt, carry):
        w_scr[...] = s1_scr[lt]
        _extract_topk(w_scr, rk1_scr, v1_scr)
        w_scr[...] = s2_scr[lt]
        _extract_topk(w_scr, rk2_scr, v2_scr)
        for p, (r1, r2) in enumerate(CAND_PAIRS):
            cw_scr[p] = v1_scr[r1] + v2_scr[r2]
        _extract_topk(cw_scr, crk_scr, cv_scr)
        top = cv_scr[0]
        zsum = jnp.zeros_like(top)
        for k in range(PEER_TOPK):
            zsum = zsum + jnp.exp(cv_scr[k] - top)
        rz = 1.0 / zsum
        rk1 = rk1_scr[...]
        n_i = jnp.zeros_like(rk1)
        for r1 in range(PEER_TOPK):
            cnt = jnp.zeros_like(top)
            for p, (a, _) in enumerate(CAND_PAIRS):
                if a == r1:
                    cnt = cnt + jnp.where(crk_scr[p] < float(PEER_TOPK), 1.0, 0.0)
            n_i = jnp.where(rk1 == float(r1), cnt[None], n_i)
        pk_scr[0] = n_i.reshape(half, LANES)
        pk_scr[1] = (jnp.exp(s1_scr[lt] - v1_scr[0][None]) * rz[None]).reshape(half, LANES)
        pk_scr[2] = rk2_scr[...].reshape(half, LANES)
        pk_scr[3] = jnp.exp(s2_scr[lt] - v2_scr[0][None]).reshape(half, LANES)
        lanes = pl.ds(pl.multiple_of(lt * LANES, LANES), LANES)
        for h in range(H):
            n_ref[h, :, lanes] = pk_scr[0, pl.ds(h, K, stride=H), :]
            e1_ref[h, :, lanes] = pk_scr[1, pl.ds(h, K, stride=H), :]
            for a, o_ref in ((2, r2_ref), (3, e2_ref)):
                lo = pk_scr[a, pl.ds(h, K // 2, stride=H), :]
                hi = pk_scr[a, pl.ds(h + (K // 2) * H, K // 2, stride=H), :]
                packed = _pack_pair(lo, hi)
                for jt in range(K // BF16_ROWS):
                    o_ref[lt, jt, h] = packed[jt * 8:(jt + 1) * 8]
        return carry

    lax.fori_loop(0, nlt, lane_tile, 0)


def _peer_select(h1, g, wq, kb1, kb2, tb):
    T, D = h1.shape
    H, K = PEER_HEADS, PEER_KEYS
    nlt = tb // LANES
    ncand = len(CAND_PAIRS)
    const = lambda shape: pl.BlockSpec(shape, lambda i: (0,) * len(shape))
    hkt = jax.ShapeDtypeStruct((H, K, T), F32)
    hkt_spec = pl.BlockSpec((H, K, tb), lambda i: (0, 0, i))
    pair = jax.ShapeDtypeStruct((T // LANES, K // BF16_ROWS, H, 8, LANES), PACKED)
    pair_spec = pl.BlockSpec((nlt, K // BF16_ROWS, H, 8, LANES), lambda i: (i, 0, 0, 0, 0))
    tile = lambda n: pltpu.VMEM((n, H, LANES), F32)
    return pl.pallas_call(
        functools.partial(_select_kernel, tb=tb),
        grid=(T // tb,),
        in_specs=[pl.BlockSpec((tb, D), lambda i: (i, 0)), const((1, D)), const(wq.shape),
                  const(kb1.shape), const(kb2.shape)],
        out_specs=[pl.BlockSpec((D, tb), lambda i: (0, i)), hkt_spec, hkt_spec, pair_spec, pair_spec],
        out_shape=[jax.ShapeDtypeStruct((D, T), BF16), hkt, hkt, pair, pair],
        scratch_shapes=[pltpu.VMEM((nlt, K, H, LANES), F32), pltpu.VMEM((nlt, K, H, LANES), F32),
                        tile(K), tile(K), tile(K), tile(PEER_TOPK), tile(PEER_TOPK),
                        tile(ncand), tile(ncand), tile(PEER_TOPK),
                        pltpu.VMEM((4, H * K, LANES), F32)],
        compiler_params=_params(("parallel",)),
        name="peer_select",
    )(h1, g, wq, kb1, kb2)


def _gelu_tanh(x):
    c = math.sqrt(2.0 / math.pi)
    return x * (0.5 * (1.0 + jnp.tanh(c * (x + 0.044715 * (x * x * x)))))


def _dense_kernel(h_ref, xnt_ref, n_ref, e1_ref, r2_ref, e2_ref, u_ref, vt_ref, o_ref,
                  pre0_scr, pre1_scr, act0_scr, act1_scr, acc_scr, *, tb, ec, nc):
    k = pl.program_id(1)
    H, K = PEER_HEADS, PEER_KEYS
    ni = ec // K
    zero = jnp.zeros((), BF16)
    lw = 2 * LANES

    @pl.when((pl.program_id(0) == 0) & (k == 0))
    def _():
        for ref in (pre0_scr, pre1_scr, act0_scr, act1_scr):
            ref[...] = jnp.zeros_like(ref)

    def stage_b(cb, pre_ref, act_ref, lp):
        for lt in range(lw // LANES):
            ltile = lp * (lw // LANES) + lt
            lanes = pl.ds(pl.multiple_of(ltile * LANES, LANES), LANES)
            for i8 in range(ni // 8):
                irows = pl.ds(pl.multiple_of(cb * ni + i8 * 8, 8), 8)
                n8 = [n_ref[h, irows, lanes] for h in range(H)]
                e8 = [e1_ref[h, irows, lanes] for h in range(H)]
                for i1 in range(8):
                    il = i8 * 8 + i1
                    nb = [jnp.broadcast_to(n8[h][i1:i1 + 1, :], (BF16_ROWS, LANES)).astype(BF16) for h in range(H)]
                    eb = [jnp.broadcast_to(e8[h][i1:i1 + 1, :], (BF16_ROWS, LANES)).astype(BF16) for h in range(H)]
                    for jt in range(K // BF16_ROWS):
                        gate = None
                        for h in range(H):
                            r = pltpu.bitcast(r2_ref[ltile, jt, h], BF16)
                            e = pltpu.bitcast(e2_ref[ltile, jt, h], BF16)
                            t = jnp.where(r < nb[h], e, zero) * eb[h]
                            gate = t if gate is None else gate + t
                        gate = pltpu.bitcast(gate, PACKED)
                        for half in range(2):
                            g = _unpack_pair(gate, half)
                            r0 = il * K + half * (K // 2) + jt * 8
                            act_ref[r0:r0 + 8, lanes] = (_gelu_tanh(pre_ref[r0:r0 + 8, lanes]) * g).astype(BF16)

    for sub in range(2):
        s = 2 * k + sub
        cb = jnp.clip(s - 1, 0, nc - 1)
        first = s == 2

        def body(lp, carry, sub=sub, cb=cb, first=first):
            lane0 = pl.multiple_of(lp * lw, lw)
            lanes = pl.ds(lane0, lw)
            pre_w, pre_r = (pre0_scr, pre1_scr) if sub == 0 else (pre1_scr, pre0_scr)
            act_r, act_w = (act0_scr, act1_scr) if sub == 0 else (act1_scr, act0_scr)
            u = pltpu.bitcast(u_ref[sub * (ec // 2):(sub + 1) * (ec // 2), :], BF16)
            vt = pltpu.bitcast(vt_ref[:, sub * ec:(sub + 1) * ec], BF16)
            pre_w[:, lanes] = _mm(u, xnt_ref[:, lanes])
            stage_b(cb, pre_r, act_w, lp)
            contrib = _mm(vt, act_r[:, lanes])
            acc_scr[:, lanes] = jnp.where(first, contrib, acc_scr[:, lanes] + contrib)
            return carry

        lax.fori_loop(0, tb // lw, body, 0)

    @pl.when(k == pl.num_programs(1) - 1)
    def _():
        o_ref[...] = h_ref[...] + acc_scr[...].T


def _pack_row_pairs(w):
    bits = lambda v: lax.bitcast_convert_type(v.astype(BF16), jnp.uint16).astype(PACKED)
    return bits(w[0::2]) | (bits(w[1::2]) << 16)


def _peer_dense(h1, xnt, n, e1, r2, e2, u, vt, tb, ec):
    T, D = h1.shape
    H, K = PEER_HEADS, PEER_KEYS
    nc = 2 * u.shape[0] // ec
    nsteps = nc // 2 + 1
    hkt_spec = pl.BlockSpec((H, K, tb), lambda i, k: (0, 0, i))
    pair_spec = pl.BlockSpec((tb // LANES, K // BF16_ROWS, H, 8, LANES), lambda i, k: (i, 0, 0, 0, 0))
    return pl.pallas_call(
        functools.partial(_dense_kernel, tb=tb, ec=ec, nc=nc),
        grid=(T // tb, nsteps),
        in_specs=[pl.BlockSpec((tb, D), lambda i, k: (i, 0)),
                  pl.BlockSpec((D, tb), lambda i, k: (0, i)),
                  hkt_spec, hkt_spec, pair_spec, pair_spec,
                  pl.BlockSpec((ec, D), lambda i, k: (jnp.minimum(k, nsteps - 2), 0)),
                  pl.BlockSpec((D // 2, 2 * ec), lambda i, k: (0, jnp.maximum(k - 1, 0)))],
        out_specs=pl.BlockSpec((tb, D), lambda i, k: (i, 0)),
        out_shape=jax.ShapeDtypeStruct((T, D), F32),
        scratch_shapes=[pltpu.VMEM((ec, tb), F32), pltpu.VMEM((ec, tb), F32),
                        pltpu.VMEM((ec, tb), BF16), pltpu.VMEM((ec, tb), BF16), pltpu.VMEM((D, tb), F32)],
        compiler_params=_params(("arbitrary", "arbitrary")),
        name="peer_dense",
    )(h1, xnt, n, e1, r2, e2, u, vt)


def _ple_kernel(h_ref, p_ref, g_ref, wg_ref, wp_ref, gf_ref, o_ref, *, final):
    h = h_ref[...]
    gate = jax.nn.sigmoid(_mm(_rms(h, g_ref[...]).astype(BF16), wg_ref[...]))
    h = h + gate * _mm(p_ref[...].astype(BF16), wp_ref[...])
    o_ref[...] = _rms(h, gf_ref[...]) if final else h


def _ple(h, p, g, wg, wp, gf, tm, final):
    T, D = h.shape
    P = p.shape[1]
    const = lambda shape: pl.BlockSpec(shape, lambda i: (0,) * len(shape))
    return pl.pallas_call(
        functools.partial(_ple_kernel, final=final),
        grid=(T // tm,),
        in_specs=[pl.BlockSpec((tm, D), lambda i: (i, 0)), pl.BlockSpec((tm, P), lambda i: (i, 0)),
                  const((1, D)), const((D, D)), const((P, D)), const((1, D))],
        out_specs=pl.BlockSpec((tm, D), lambda i: (i, 0)),
        out_shape=jax.ShapeDtypeStruct((T, D), F32),
        compiler_params=_params(("parallel",)),
        name="ple",
    )(h, p, g, wg, wp, gf)


def _tiles(T):
    tm = next(t for t in (512, 256, 128, 64, 32, 16, 8) if T % t == 0)
    tb = next(t for t in (512, 256) if T % t == 0)
    return tm, tb


def _pad_lanes(v, fill=0.0):
    return jnp.pad(v.astype(F32), (0, LANES - v.shape[0]), constant_values=fill)[None, :]


def kernel(x_prompt, x_sample, state_ssd, state_ssd_conv, state_cf_conv, p_prompt, p_sample, g_mix, w_in, ssd_conv_w, ssd_conv_b, ssd_dt_bias, ssd_a_log, ssd_d, ssd_norm_g, w_ssd_out, cf_dw_w, cf_dw_b, cf_ln_g, cf_ln_b, w_cf_out, w_o, g_ffn, peer_wq, peer_keys, peer_u, peer_v, g_ple, w_ple_gate, w_ple_proj, g_final):
    depth = w_in.shape[0]
    bp, lp, D = x_prompt.shape
    bs, ls, _ = x_sample.shape
    tp, ts = bp * lp, bs * ls
    T = tp + ts
    inner = SSD_HEADS * SSD_HEAD_DIM
    cd = state_ssd_conv.shape[-1]
    H, K = PEER_HEADS, PEER_KEYS
    tm, tb = _tiles(T)
    lc_p, lc_s = min(SSD_CHUNK, lp), min(SSD_CHUNK, ls)
    assert lp % lc_p == 0 and ls % lc_s == 0 and tp % lc_s == 0 and D == inner == cf_dw_w.shape[-1]

    h = jnp.concatenate([x_prompt.reshape(tp, D), x_sample.reshape(ts, D)], axis=0)
    expand = (jnp.arange(inner)[None, :] // SSD_HEAD_DIM == jnp.arange(LANES)[:, None]).astype(F32)
    zeros_conv = jnp.zeros((bp, SSD_CONV - 1, cd), F32)
    zeros_h = jnp.zeros((bp, SSD_HEADS, SSD_HEAD_DIM, SSD_STATE), F32)
    zeros_cf = jnp.zeros((bp, CF_CONV - 1, D), F32)
    row = lambda v: v.astype(F32)[None, :]
    eye_h = jnp.eye(H, dtype=F32)
    outs = {k: [] for k in ("ssd_p", "sconv_p", "cf_p", "ssd_s", "sconv_s", "cf_s")}

    for i in range(depth):
        w = w_in[i]
        o_xbc, o_dt = inner, inner + cd
        o_glu = o_dt + SSD_HEADS
        w_r = jnp.concatenate([w[:, :o_xbc], w[:, o_xbc:o_dt], w[:, o_glu:],
                               jnp.pad(w[:, o_dt:o_glu], ((0, 0), (0, LANES - SSD_HEADS)))], axis=1).astype(BF16)
        ncols = w_r.shape[1]
        tn = next(t for t in (ncols // 3, ncols) if t % LANES == 0 and ncols % t == 0)
        proj = _inproj(h, row(g_mix[i]), w_r, tm, tn)

        ssd_w = (ssd_conv_w[i], row(ssd_conv_b[i]), _pad_lanes(ssd_dt_bias[i]), _pad_lanes(ssd_a_log[i]),
                 row(jnp.repeat(ssd_d[i], SSD_HEAD_DIM)), row(ssd_norm_g[i]), expand, expand.T)
        ya_p, sconv_p, ssd_p = _ssd(proj, zeros_conv, zeros_h, ssd_w, nb=bp, nc=lp // lc_p, L=lc_p, row_off=0)
        ya_s, sconv_s, ssd_s = _ssd(proj, state_ssd_conv[i], state_ssd[i], ssd_w,
                                    nb=bs, nc=ls // lc_s, L=lc_s, row_off=tp)
        cf_w = (cf_dw_w[i], row(cf_dw_b[i]), row(cf_ln_g[i]), row(cf_ln_b[i]))
        yb_p, cf_p = _conformer(proj, zeros_cf, cf_w, nb=bp, nc=lp // lc_p, L=lc_p, row_off=0)
        yb_s, cf_s = _conformer(proj, state_cf_conv[i], cf_w, nb=bs, nc=ls // lc_s, L=lc_s, row_off=tp)
        for k, v in (("ssd_p", ssd_p), ("sconv_p", sconv_p), ("cf_p", cf_p),
                     ("ssd_s", ssd_s), ("sconv_s", sconv_s), ("cf_s", cf_s)):
            outs[k].append(v)
        ya = jnp.concatenate([ya_p, ya_s], axis=0)
        yb = jnp.concatenate([yb_p, yb_s], axis=0)
        h = _merge(h, ya, yb, proj, w_ssd_out[i].astype(BF16), w_cf_out[i].astype(BF16), w_o[i].astype(BF16), tm)

        qd = peer_keys.shape[-1]
        wq = peer_wq[i].reshape(D, H, 2, qd).transpose(0, 2, 1, 3).reshape(D, 2 * H * qd).astype(BF16)
        kb = [jnp.einsum("hkd,hg->khgd", peer_keys[i, :, s], eye_h).reshape(K * H, H * qd).astype(BF16)
              for s in range(2)]
        xnt, n_sel, e1, r2, e2 = _peer_select(h, row(g_ffn[i]), wq, kb[0], kb[1], tb)
        h = _peer_dense(h, xnt, n_sel, e1, r2, e2, _pack_row_pairs(peer_u[i]), _pack_row_pairs(peer_v[i].T),
                        tb, ec=8 * K)

        p_all = jnp.concatenate([p_prompt[i].reshape(tp, -1), p_sample[i].reshape(ts, -1)], axis=0)
        h = _ple(h, p_all, row(g_ple[i]), w_ple_gate[i].astype(BF16), w_ple_proj[i].astype(BF16),
                 row(g_final), tm, final=(i == depth - 1))

    y_prompt = h[:tp].reshape(bp, lp, D)
    y_sample = h[tp:].reshape(bs, ls, D)
    st = {k: jnp.stack(v) for k, v in outs.items()}
    return (y_prompt, y_sample, st["ssd_p"], st["sconv_p"], st["cf_p"], st["ssd_s"], st["sconv_s"], st["cf_s"])
```

```python
import functools
import math

import jax
import jax.numpy as jnp
from jax import lax
from jax.experimental import pallas as pl
from jax.experimental.pallas import tpu as pltpu

F32, BF16 = jnp.float32, jnp.bfloat16
PACKED = jnp.uint32
EPS = 1e-6
HIGHEST = lax.Precision.HIGHEST

SSD_HEADS, SSD_HEAD_DIM, SSD_GROUPS, SSD_STATE, SSD_CONV, SSD_CHUNK = 16, 64, 4, 128, 4, 128
CF_CONV = 31
PEER_HEADS, PEER_KEYS, PEER_TOPK = 8, 128, 16
LANES = 128
BF16_ROWS = 16
VMEM_LIMIT = 56 * 1024 * 1024

CAND_PAIRS = [(r1, r2) for r1 in range(PEER_TOPK) for r2 in range(PEER_TOPK)
              if (r1 + 1) * (r2 + 1) <= PEER_TOPK]


def _nt(a, b, **kw):
    return lax.dot_general(a, b, (((1,), (1,)), ((), ())), preferred_element_type=F32, **kw)


def _tn(a, b, **kw):
    return lax.dot_general(a, b, (((0,), (0,)), ((), ())), preferred_element_type=F32, **kw)


def _mm(a, b, **kw):
    return jnp.dot(a, b, preferred_element_type=F32, **kw)


def _rms(x, g):
    return x * lax.rsqrt(jnp.mean(x * x, axis=-1, keepdims=True) + EPS) * g


def _silu(x):
    return x * jax.nn.sigmoid(x)


def _pack_pair(lo, hi):
    bits = lambda v: lax.bitcast_convert_type(v.astype(BF16).astype(F32), PACKED)
    return (bits(hi) & jnp.uint32(0xFFFF0000)) | lax.shift_right_logical(bits(lo), jnp.uint32(16))


def _unpack_pair(packed, half):
    bits = (packed & jnp.uint32(0xFFFF0000)) if half else lax.shift_left(packed, jnp.uint32(16))
    return lax.bitcast_convert_type(bits, F32)


def _params(sem, flags=None):
    return pltpu.CompilerParams(dimension_semantics=sem, vmem_limit_bytes=VMEM_LIMIT, flags=flags)


def _inproj_kernel(x_ref, g_ref, w_ref, o_ref, xn_ref):
    @pl.when(pl.program_id(1) == 0)
    def _():
        xn_ref[...] = _rms(x_ref[...], g_ref[...]).astype(BF16)

    o_ref[...] = _mm(xn_ref[...], w_ref[...])


def _inproj(h, g, w, tm, tn):
    T, D = h.shape
    N = w.shape[1]
    return pl.pallas_call(
        _inproj_kernel,
        grid=(T // tm, N // tn),
        in_specs=[pl.BlockSpec((tm, D), lambda i, j: (i, 0)),
                  pl.BlockSpec((1, D), lambda i, j: (0, 0)),
                  pl.BlockSpec((D, tn), lambda i, j: (0, j))],
        out_specs=pl.BlockSpec((tm, tn), lambda i, j: (i, j)),
        out_shape=jax.ShapeDtypeStruct((T, N), F32),
        scratch_shapes=[pltpu.VMEM((tm, D), BF16)],
        compiler_params=_params(("parallel", "arbitrary")),
        name="inproj",
    )(h, g, w)


def _ssd_kernel(z_ref, xs_ref, bc_ref, dt_ref, conv0_ref, h0_ref, cw_ref, cb_ref, dtb_ref, alog_ref,
                dexp_ref, ng_ref, e_ref, et_ref,
                y_ref, nconv_ref, ht_ref,
                xpad_ref, st_ref, yscr_ref, *, L):
    c = pl.program_id(1)
    inner = SSD_HEADS * SSD_HEAD_DIM
    hist = SSD_CONV - 1
    base = 8

    @pl.when(c == 0)
    def _():
        xpad_ref[base - hist:base, :] = conv0_ref[0]
        st_ref[...] = h0_ref[0]

    xpad_ref[base:base + L, 0:inner] = xs_ref[...]
    xpad_ref[base:base + L, inner:] = bc_ref[...]

    conv = cb_ref[...] + cw_ref[0:1, :] * xpad_ref[base - hist:base - hist + L, :]
    for k in range(1, SSD_CONV):
        conv = conv + cw_ref[k:k + 1, :] * xpad_ref[base - hist + k:base - hist + k + L, :]
    last_rows = xpad_ref[base + L - hist:base + L, :]
    nconv_ref[0] = last_rows
    xpad_ref[base - hist:base, :] = last_rows
    xc = _silu(conv)
    xs = xc[:, :inner]
    gn = SSD_GROUPS * SSD_STATE
    bm = xc[:, inner:inner + gn].astype(BF16)
    cm = xc[:, inner + gn:].astype(BF16)

    x = dt_ref[...] + dtb_ref[...]
    dt = jnp.maximum(x, 0.0) + jnp.log1p(jnp.exp(-jnp.abs(x)))
    da = dt * (-jnp.exp(alog_ref[...]))
    row = lax.broadcasted_iota(jnp.int32, (L, L), 0)
    col = lax.broadcasted_iota(jnp.int32, (L, L), 1)
    causal = row >= col
    acs = _mm(causal.astype(F32), da, precision=HIGHEST)
    e = e_ref[...]
    dt_x = _mm(dt, e, precision=HIGHEST)
    acs_x = _mm(acs, e, precision=HIGHEST)
    acs_t = _nt(et_ref[...], acs, precision=HIGHEST)
    alast_x = acs_x[L - 1:L, :]
    xd = xs * dt_x
    xd_bf = xd.astype(BF16)
    xdec = (xd * jnp.exp(alast_x - acs_x)).astype(BF16)
    eacs = jnp.exp(acs_x)
    lane = lax.broadcasted_iota(jnp.int32, (L, 2 * SSD_HEAD_DIM), 1)

    heads_per_group = SSD_HEADS // SSD_GROUPS
    for g in range(SSD_GROUPS):
        bg = bm[:, g * SSD_STATE:(g + 1) * SSD_STATE]
        cg = cm[:, g * SSD_STATE:(g + 1) * SSD_STATE]
        cb = _nt(cg, bg)
        for pp in range(heads_per_group // 2):
            p = g * (heads_per_group // 2) + pp
            cols = slice(p * 2 * SSD_HEAD_DIM, (p + 1) * 2 * SSD_HEAD_DIM)
            xp = xd_bf[:, cols]
            ys = []
            for hh in range(2):
                h = 2 * p + hh
                acol = acs[:, h:h + 1]
                arow = acs_t[h * SSD_HEAD_DIM:h * SSD_HEAD_DIM + 1, :]
                lm = jnp.where(causal, jnp.exp(acol - arow), 0.0)
                ys.append(_mm((cb * lm).astype(BF16), xp))
            ydiag = jnp.where(lane < SSD_HEAD_DIM, ys[0], ys[1])
            st = st_ref[p]
            yoff = _nt(cg, st.astype(BF16)) * eacs[:, cols]
            yscr_ref[:, cols] = ydiag + yoff
            scale = jnp.exp(acs_t[p * 2 * SSD_HEAD_DIM:(p + 1) * 2 * SSD_HEAD_DIM, L - 1:L])
            st_ref[p] = st * scale + _tn(xdec[:, cols], bg)

    ht_ref[0] = st_ref[...]

    y = (yscr_ref[...] + dexp_ref[...] * xs) * _silu(z_ref[...])
    gw = inner // SSD_GROUPS
    for g in range(SSD_GROUPS):
        seg = y[:, g * gw:(g + 1) * gw]
        seg = seg * lax.rsqrt(jnp.mean(seg * seg, axis=-1, keepdims=True) + EPS)
        y_ref[:, g * gw:(g + 1) * gw] = seg * ng_ref[:, g * gw:(g + 1) * gw]


def _ssd(proj, conv0, h0, wts, *, nb, nc, L, row_off):
    inner = SSD_HEADS * SSD_HEAD_DIM
    cd = conv0.shape[-1]
    r0 = row_off // L
    rows = lambda b, c: r0 + b * nc + c
    dt_blk = (proj.shape[1] - LANES) // LANES
    const = lambda shape: pl.BlockSpec(shape, lambda b, c: (0,) * len(shape))
    npair = SSD_HEADS // 2
    y, nconv, ht = pl.pallas_call(
        functools.partial(_ssd_kernel, L=L),
        grid=(nb, nc),
        in_specs=[pl.BlockSpec((L, inner), lambda b, c: (rows(b, c), 0)),
                  pl.BlockSpec((L, inner), lambda b, c: (rows(b, c), 1)),
                  pl.BlockSpec((L, inner), lambda b, c: (rows(b, c), 2)),
                  pl.BlockSpec((L, LANES), lambda b, c: (rows(b, c), dt_blk)),
                  pl.BlockSpec((1, SSD_CONV - 1, cd), lambda b, c: (b, 0, 0)),
                  pl.BlockSpec((1, npair, LANES, SSD_STATE), lambda b, c: (b, 0, 0, 0)),
                  const((SSD_CONV, cd)), const((1, cd)), const((1, LANES)), const((1, LANES)),
                  const((1, inner)), const((1, inner)), const((LANES, inner)), const((inner, LANES))],
        out_specs=[pl.BlockSpec((L, inner), lambda b, c: (b * nc + c, 0)),
                   pl.BlockSpec((1, SSD_CONV - 1, cd), lambda b, c: (b, 0, 0)),
                   pl.BlockSpec((1, npair, LANES, SSD_STATE), lambda b, c: (b, 0, 0, 0))],
        out_shape=[jax.ShapeDtypeStruct((nb * nc * L, inner), F32),
                   jax.ShapeDtypeStruct((nb, SSD_CONV - 1, cd), F32),
                   jax.ShapeDtypeStruct((nb, npair, LANES, SSD_STATE), F32)],
        scratch_shapes=[pltpu.VMEM((8 + L, cd), F32),
                        pltpu.VMEM((npair, LANES, SSD_STATE), F32),
                        pltpu.VMEM((L, inner), F32)],
        compiler_params=_params(("parallel", "arbitrary")),
        name=f"ssd_L{L}",
    )(proj, proj, proj, proj, conv0, h0.reshape(nb, npair, LANES, SSD_STATE), *wts)
    return y, nconv, ht.reshape(nb, SSD_HEADS, SSD_HEAD_DIM, SSD_STATE)


def _cf_kernel(a_ref, g_ref, buf0_ref, w_ref, b_ref, lng_ref, lnb_ref, y_ref, nbuf_ref,
               upad_ref, conv_ref, *, L):
    c = pl.program_id(1)
    hist = CF_CONV - 1
    base = 32
    dim = a_ref.shape[1]

    @pl.when(c == 0)
    def _():
        upad_ref[base - hist:base, :] = buf0_ref[0]

    upad_ref[base:base + L, :] = a_ref[...] * jax.nn.sigmoid(g_ref[...])
    slab = 2 * LANES
    for s in range(dim // slab):
        cols = slice(s * slab, (s + 1) * slab)
        acc = b_ref[:, cols] + w_ref[0:1, cols] * upad_ref[base - hist:base - hist + L, cols]
        for k in range(1, CF_CONV):
            acc = acc + w_ref[k:k + 1, cols] * upad_ref[base - hist + k:base - hist + k + L, cols]
        conv_ref[:, cols] = acc
    nbuf = upad_ref[base + L - hist:base + L, :]
    nbuf_ref[0] = nbuf
    upad_ref[base - hist:base, :] = nbuf

    x = conv_ref[...]
    xc = x - jnp.mean(x, axis=-1, keepdims=True)
    y = xc * lax.rsqrt(jnp.mean(xc * xc, axis=-1, keepdims=True) + EPS) * lng_ref[...] + lnb_ref[...]
    y_ref[...] = _silu(y)


def _conformer(proj, buf0, wts, *, nb, nc, L, row_off):
    dim = buf0.shape[-1]
    r0 = row_off // L
    rows = lambda b, c: r0 + b * nc + c
    const = lambda shape: pl.BlockSpec(shape, lambda b, c: (0,) * len(shape))
    return pl.pallas_call(
        functools.partial(_cf_kernel, L=L),
        grid=(nb, nc),
        in_specs=[pl.BlockSpec((L, dim), lambda b, c: (rows(b, c), 3)),
                  pl.BlockSpec((L, dim), lambda b, c: (rows(b, c), 4)),
                  pl.BlockSpec((1, CF_CONV - 1, dim), lambda b, c: (b, 0, 0)),
                  const((CF_CONV, dim)), const((1, dim)), const((1, dim)), const((1, dim))],
        out_specs=[pl.BlockSpec((L, dim), lambda b, c: (b * nc + c, 0)),
                   pl.BlockSpec((1, CF_CONV - 1, dim), lambda b, c: (b, 0, 0))],
        out_shape=[jax.ShapeDtypeStruct((nb * nc * L, dim), F32),
                   jax.ShapeDtypeStruct((nb, CF_CONV - 1, dim), F32)],
        scratch_shapes=[pltpu.VMEM((32 + L, dim), F32), pltpu.VMEM((L, dim), F32)],
        compiler_params=_params(("parallel", "arbitrary")),
        name=f"conformer_L{L}",
    )(proj, proj, buf0, *wts)


def _merge_kernel(h_ref, ya_ref, yb_ref, ga_ref, gb_ref, wa_ref, wb_ref, wo_ref, o_ref):
    mix = (jax.nn.sigmoid(ga_ref[...]) * _mm(ya_ref[...].astype(BF16), wa_ref[...])
           + jax.nn.sigmoid(gb_ref[...]) * _mm(yb_ref[...].astype(BF16), wb_ref[...]))
    o_ref[...] = h_ref[...] + _mm(mix.astype(BF16), wo_ref[...])


def _merge(h, ya, yb, proj, wa, wb, wo, tm):
    T, D = h.shape
    rowblk = lambda j: pl.BlockSpec((tm, D), lambda i: (i, j))
    wspec = pl.BlockSpec((D, D), lambda i: (0, 0))
    return pl.pallas_call(
        _merge_kernel,
        grid=(T // tm,),
        in_specs=[rowblk(0), rowblk(0), rowblk(0), rowblk(5), rowblk(6), wspec, wspec, wspec],
        out_specs=rowblk(0),
        out_shape=jax.ShapeDtypeStruct((T, D), F32),
        compiler_params=_params(("parallel",)),
        name="merge",
    )(h, ya, yb, proj, proj, wa, wb, wo)


def _extract_topk(s_ref, rank_ref, vals_ref):
    rank_ref[...] = jnp.full(rank_ref.shape, float(PEER_TOPK), F32)
    n = s_ref.shape[0]

    def step(k, carry):
        s = s_ref[...]
        m = jnp.max(s, axis=0)
        idx = lax.broadcasted_iota(jnp.int32, s.shape, 0)
        first = jnp.min(jnp.where(s == m[None], idx, n), axis=0)
        hit = idx == first[None]
        rank_ref[...] = jnp.where(hit, jnp.asarray(k, F32), rank_ref[...])
        s_ref[...] = jnp.where(hit, -jnp.inf, s)
        vals_ref[k] = m
        return carry

    lax.fori_loop(0, PEER_TOPK, step, 0)


def _sort16_desc(x):
    x = list(x)
    n = len(x)
    k = 2
    while k <= n:
        j = k // 2
        while j >= 1:
            for i in range(n):
                l = i ^ j
                if l > i:
                    hi, lo = jnp.maximum(x[i], x[l]), jnp.minimum(x[i], x[l])
                    x[i], x[l] = (hi, lo) if (i & k) == 0 else (lo, hi)
            j //= 2
        k *= 2
    return x


def _merge_top16(a, b):
    n = len(a)
    x = [jnp.maximum(a[i], b[n - 1 - i]) for i in range(n)]
    j = n // 2
    while j >= 1:
        for i in range(n):
            l = i ^ j
            if l > i:
                x[i], x[l] = jnp.maximum(x[i], x[l]), jnp.minimum(x[i], x[l])
        j //= 2
    return x


def _top16_sorted(rows):
    groups = [_sort16_desc(rows[g:g + PEER_TOPK]) for g in range(0, len(rows), PEER_TOPK)]
    while len(groups) > 1:
        groups = [_merge_top16(groups[i], groups[i + 1]) for i in range(0, len(groups), 2)]
    return groups[0]


def _select_kernel(h_ref, g_ref, wq_ref, kb1_ref, kb2_ref,
                   xnt_ref, n_ref, e1_ref, r2_ref, e2_ref,
                   s1_scr, s2_scr, w_scr, rk1_scr, rk2_scr, v1_scr, v2_scr,
                   cw_scr, crk_scr, cv_scr, pk_scr, *, tb):
    H, K = PEER_HEADS, PEER_KEYS
    half = H * K
    xn = _rms(h_ref[...], g_ref[...])
    xnt_ref[...] = xn.T.astype(BF16)
    q = _mm(xn.astype(BF16), wq_ref[...]).astype(BF16)
    s1 = _nt(kb1_ref[...], q[:, :half])
    s2 = _nt(kb2_ref[...], q[:, half:])
    nlt = tb // LANES
    for lt in range(nlt):
        s1_scr[lt] = s1[:, lt * LANES:(lt + 1) * LANES].reshape(K, H, LANES)
        s2_scr[lt] = s2[:, lt * LANES:(lt + 1) * LANES].reshape(K, H, LANES)

    topk = float(PEER_TOPK)

    def by_value(lt):
        a = _top16_sorted([s1_scr[lt, k] for k in range(K)])
        b = _top16_sorted([s2_scr[lt, k] for k in range(K)])
        cands = [a[r1] + b[r2] for r1, r2 in CAND_PAIRS]
        pad = [jnp.full((H, LANES), -jnp.inf, F32)] * (-len(cands) % PEER_TOPK)
        t = _top16_sorted(cands + pad)
        won = [jnp.where(c >= t[-1], 1.0, 0.0) for c in cands]
        ncnt = [sum(w for w, (r1, _) in zip(won, CAND_PAIRS) if r1 == r) for r in range(PEER_TOPK)]
        tie = jnp.where(sum(won) != topk, 1.0, 0.0)
        for v in (a, b, t):
            for k in range(PEER_TOPK - 1):
                tie = jnp.where(v[k] == v[k + 1], 1.0, tie)
        rz = 1.0 / sum(jnp.exp(v - t[0]) for v in t)

        def keys0(i, cnt):
            s = s1_scr[lt, i]
            n_i = jnp.zeros_like(s)
            for r in range(PEER_TOPK):
                n_i = jnp.where(s == a[r], ncnt[r], n_i)
            rows = pl.ds(pl.multiple_of(i * H, H), H)
            pk_scr[0, rows, :] = n_i
            pk_scr[1, rows, :] = jnp.exp(s - a[0]) * rz
            return cnt + jnp.where(s >= a[-1], 1.0, 0.0)

        def keys1(j, cnt):
            s = s2_scr[lt, j]
            rank = jnp.full_like(s, topk)
            for r in range(PEER_TOPK):
                rank = jnp.where(s == b[r], float(r), rank)
            rows = pl.ds(pl.multiple_of(j * H, H), H)
            pk_scr[2, rows, :] = rank
            pk_scr[3, rows, :] = jnp.exp(s - b[0])
            return cnt + jnp.where(s >= b[-1], 1.0, 0.0)

        zero = jnp.zeros((H, LANES), F32)
        for body in (keys0, keys1):
            tie = jnp.where(lax.fori_loop(0, K, body, zero, unroll=8) != topk, 1.0, tie)
        return tie

    def by_extraction(lt):
        w_scr[...] = s1_scr[lt]
        _extract_topk(w_scr, rk1_scr, v1_scr)
        w_scr[...] = s2_scr[lt]
        _extract_topk(w_scr, rk2_scr, v2_scr)
        for p, (r1, r2) in enumerate(CAND_PAIRS):
            cw_scr[p] = v1_scr[r1] + v2_scr[r2]
        _extract_topk(cw_scr, crk_scr, cv_scr)
        top = cv_scr[0]
        zsum = jnp.zeros_like(top)
        for k in range(PEER_TOPK):
            zsum = zsum + jnp.exp(cv_scr[k] - top)
        rz = 1.0 / zsum
        rk1 = rk1_scr[...]
        n_i = jnp.zeros_like(rk1)
        for r1 in range(PEER_TOPK):
            cnt = jnp.zeros_like(top)
            for p, (a, _) in enumerate(CAND_PAIRS):
                if a == r1:
                    cnt = cnt + jnp.where(crk_scr[p] < topk, 1.0, 0.0)
            n_i = jnp.where(rk1 == float(r1), cnt[None], n_i)
        pk_scr[0] = n_i.reshape(half, LANES)
        pk_scr[1] = (jnp.exp(s1_scr[lt] - v1_scr[0][None]) * rz[None]).reshape(half, LANES)
        pk_scr[2] = rk2_scr[...].reshape(half, LANES)
        pk_scr[3] = jnp.exp(s2_scr[lt] - v2_scr[0][None]).reshape(half, LANES)

    def lane_tile(lt, carry):
        tie = by_value(lt)

        @pl.when(jnp.max(tie) > 0.0)
        def _():
            by_extraction(lt)

        lanes = pl.ds(pl.multiple_of(lt * LANES, LANES), LANES)
        for h in range(H):
            n_ref[h, :, lanes] = pk_scr[0, pl.ds(h, K, stride=H), :]
            e1_ref[h, :, lanes] = pk_scr[1, pl.ds(h, K, stride=H), :]
            for a, o_ref in ((2, r2_ref), (3, e2_ref)):
                lo = pk_scr[a, pl.ds(h, K // 2, stride=H), :]
                hi = pk_scr[a, pl.ds(h + (K // 2) * H, K // 2, stride=H), :]
                packed = _pack_pair(lo, hi)
                for jt in range(K // BF16_ROWS):
                    o_ref[lt, jt, h] = packed[jt * 8:(jt + 1) * 8]
        return carry

    lax.fori_loop(0, nlt, lane_tile, 0)


def _peer_select(h1, g, wq, kb1, kb2, tb):
    T, D = h1.shape
    H, K = PEER_HEADS, PEER_KEYS
    nlt = tb // LANES
    ncand = len(CAND_PAIRS)
    const = lambda shape: pl.BlockSpec(shape, lambda i: (0,) * len(shape))
    hkt = jax.ShapeDtypeStruct((H, K, T), F32)
    hkt_spec = pl.BlockSpec((H, K, tb), lambda i: (0, 0, i))
    pair = jax.ShapeDtypeStruct((T // LANES, K // BF16_ROWS, H, 8, LANES), PACKED)
    pair_spec = pl.BlockSpec((nlt, K // BF16_ROWS, H, 8, LANES), lambda i: (i, 0, 0, 0, 0))
    tile = lambda n: pltpu.VMEM((n, H, LANES), F32)
    return pl.pallas_call(
        functools.partial(_select_kernel, tb=tb),
        grid=(T // tb,),
        in_specs=[pl.BlockSpec((tb, D), lambda i: (i, 0)), const((1, D)), const(wq.shape),
                  const(kb1.shape), const(kb2.shape)],
        out_specs=[pl.BlockSpec((D, tb), lambda i: (0, i)), hkt_spec, hkt_spec, pair_spec, pair_spec],
        out_shape=[jax.ShapeDtypeStruct((D, T), BF16), hkt, hkt, pair, pair],
        scratch_shapes=[pltpu.VMEM((nlt, K, H, LANES), F32), pltpu.VMEM((nlt, K, H, LANES), F32),
                        tile(K), tile(K), tile(K), tile(PEER_TOPK), tile(PEER_TOPK),
                        tile(ncand), tile(ncand), tile(PEER_TOPK),
                        pltpu.VMEM((4, H * K, LANES), F32)],
        compiler_params=_params(("parallel",)),
        name="peer_select",
    )(h1, g, wq, kb1, kb2)


def _gelu_tanh(x):
    c = math.sqrt(2.0 / math.pi)
    hx = 0.5 * x
    return hx * jnp.tanh(x * (c + (c * 0.044715) * (x * x))) + hx


def _dense_kernel(h_ref, xnt_ref, n_ref, e1_ref, r2_ref, e2_ref, u_ref, vt_ref, o_ref,
                  pre0_scr, pre1_scr, act0_scr, act1_scr, acc_scr, *, tb, ec, nc):
    k = pl.program_id(1)
    H, K = PEER_HEADS, PEER_KEYS
    ni = ec // K
    zero = jnp.zeros((), BF16)
    lw = 2 * LANES

    @pl.when((pl.program_id(0) == 0) & (k == 0))
    def _():
        for ref in (pre0_scr, pre1_scr, act0_scr, act1_scr):
            ref[...] = jnp.zeros_like(ref)

    def stage_b(cb, pre_ref, act_ref, lp):
        for lt in range(lw // LANES):
            ltile = lp * (lw // LANES) + lt
            lanes = pl.ds(pl.multiple_of(ltile * LANES, LANES), LANES)
            for i8 in range(ni // 8):
                irows = pl.ds(pl.multiple_of(cb * ni + i8 * 8, 8), 8)
                n8 = [n_ref[h, irows, lanes] for h in range(H)]
                e8 = [e1_ref[h, irows, lanes] for h in range(H)]
                for i1 in range(8):
                    il = i8 * 8 + i1
                    nb = [jnp.broadcast_to(n8[h][i1:i1 + 1, :], (BF16_ROWS, LANES)).astype(BF16) for h in range(H)]
                    eb = [jnp.broadcast_to(e8[h][i1:i1 + 1, :], (BF16_ROWS, LANES)).astype(BF16) for h in range(H)]
                    for jt in range(K // BF16_ROWS):
                        gate = None
                        for h in range(H):
                            r = pltpu.bitcast(r2_ref[ltile, jt, h], BF16)
                            e = pltpu.bitcast(e2_ref[ltile, jt, h], BF16)
                            t = jnp.where(r < nb[h], e, zero) * eb[h]
                            gate = t if gate is None else gate + t
                        gate = pltpu.bitcast(gate, PACKED)
                        for half in range(2):
                            g = _unpack_pair(gate, half)
                            r0 = il * K + half * (K // 2) + jt * 8
                            act_ref[r0:r0 + 8, lanes] = (_gelu_tanh(pre_ref[r0:r0 + 8, lanes]) * g).astype(BF16)

    for sub in range(2):
        s = 2 * k + sub
        cb = jnp.clip(s - 1, 0, nc - 1)
        first = s == 2

        def body(lp, carry, sub=sub, cb=cb, first=first):
            lane0 = pl.multiple_of(lp * lw, lw)
            lanes = pl.ds(lane0, lw)
            pre_w, pre_r = (pre0_scr, pre1_scr) if sub == 0 else (pre1_scr, pre0_scr)
            act_r, act_w = (act0_scr, act1_scr) if sub == 0 else (act1_scr, act0_scr)
            u = pltpu.bitcast(u_ref[sub * (ec // 2):(sub + 1) * (ec // 2), :], BF16)
            vt = pltpu.bitcast(vt_ref[:, sub * ec:(sub + 1) * ec], BF16)
            pre_w[:, lanes] = _mm(u, xnt_ref[:, lanes])
            stage_b(cb, pre_r, act_w, lp)
            contrib = _mm(vt, act_r[:, lanes])
            acc_scr[:, lanes] = jnp.where(first, contrib, acc_scr[:, lanes] + contrib)
            return carry

        lax.fori_loop(0, tb // lw, body, 0)

    @pl.when(k == pl.num_programs(1) - 1)
    def _():
        o_ref[...] = h_ref[...] + acc_scr[...].T


def _pack_kernel(w_ref, o_ref, *, transpose):
    w = w_ref[...]
    o_ref[...] = pltpu.bitcast((w.T if transpose else w).astype(BF16), PACKED)


def _pack_table(w, *, transpose, rows=1024):
    E, D = w.shape
    if transpose:
        out_shape, out_spec = (D // 2, E), pl.BlockSpec((D // 2, rows), lambda i: (0, i))
    else:
        out_shape, out_spec = (E // 2, D), pl.BlockSpec((rows // 2, D), lambda i: (i, 0))
    return pl.pallas_call(
        functools.partial(_pack_kernel, transpose=transpose),
        grid=(E // rows,),
        in_specs=[pl.BlockSpec((rows, D), lambda i: (i, 0))],
        out_specs=out_spec,
        out_shape=jax.ShapeDtypeStruct(out_shape, PACKED),
        compiler_params=_params(("parallel",)),
        name="pack_table_t" if transpose else "pack_table",
    )(w)


def _peer_dense(h1, xnt, n, e1, r2, e2, u, vt, tb, ec):
    T, D = h1.shape
    H, K = PEER_HEADS, PEER_KEYS
    nc = 2 * u.shape[0] // ec
    nsteps = nc // 2 + 1
    hkt_spec = pl.BlockSpec((H, K, tb), lambda i, k: (0, 0, i))
    pair_spec = pl.BlockSpec((tb // LANES, K // BF16_ROWS, H, 8, LANES), lambda i, k: (i, 0, 0, 0, 0))
    return pl.pallas_call(
        functools.partial(_dense_kernel, tb=tb, ec=ec, nc=nc),
        grid=(T // tb, nsteps),
        in_specs=[pl.BlockSpec((tb, D), lambda i, k: (i, 0)),
                  pl.BlockSpec((D, tb), lambda i, k: (0, i)),
                  hkt_spec, hkt_spec, pair_spec, pair_spec,
                  pl.BlockSpec((ec, D), lambda i, k: (jnp.minimum(k, nsteps - 2), 0)),
                  pl.BlockSpec((D // 2, 2 * ec), lambda i, k: (0, jnp.maximum(k - 1, 0)))],
        out_specs=pl.BlockSpec((tb, D), lambda i, k: (i, 0)),
        out_shape=jax.ShapeDtypeStruct((T, D), F32),
        scratch_shapes=[pltpu.VMEM((ec, tb), F32), pltpu.VMEM((ec, tb), F32),
                        pltpu.VMEM((ec, tb), BF16), pltpu.VMEM((ec, tb), BF16), pltpu.VMEM((D, tb), F32)],
        compiler_params=_params(("arbitrary", "arbitrary")),
        name="peer_dense",
    )(h1, xnt, n, e1, r2, e2, u, vt)


def _ple_kernel(h_ref, p_ref, g_ref, wg_ref, wp_ref, gf_ref, o_ref, *, final):
    h = h_ref[...]
    gate = jax.nn.sigmoid(_mm(_rms(h, g_ref[...]).astype(BF16), wg_ref[...]))
    h = h + gate * _mm(p_ref[...].astype(BF16), wp_ref[...])
    o_ref[...] = _rms(h, gf_ref[...]) if final else h


def _ple(h, p, g, wg, wp, gf, tm, final):
    T, D = h.shape
    P = p.shape[1]
    const = lambda shape: pl.BlockSpec(shape, lambda i: (0,) * len(shape))
    return pl.pallas_call(
        functools.partial(_ple_kernel, final=final),
        grid=(T // tm,),
        in_specs=[pl.BlockSpec((tm, D), lambda i: (i, 0)), pl.BlockSpec((tm, P), lambda i: (i, 0)),
                  const((1, D)), const((D, D)), const((P, D)), const((1, D))],
        out_specs=pl.BlockSpec((tm, D), lambda i: (i, 0)),
        out_shape=jax.ShapeDtypeStruct((T, D), F32),
        compiler_params=_params(("parallel",)),
        name="ple",
    )(h, p, g, wg, wp, gf)


def _tiles(T):
    tm = next(t for t in (512, 256, 128, 64, 32, 16, 8) if T % t == 0)
    tb = next(t for t in (512, 256) if T % t == 0)
    return tm, tb


def _pad_lanes(v, fill=0.0):
    return jnp.pad(v.astype(F32), (0, LANES - v.shape[0]), constant_values=fill)[None, :]


def kernel(x_prompt, x_sample, state_ssd, state_ssd_conv, state_cf_conv, p_prompt, p_sample, g_mix, w_in, ssd_conv_w, ssd_conv_b, ssd_dt_bias, ssd_a_log, ssd_d, ssd_norm_g, w_ssd_out, cf_dw_w, cf_dw_b, cf_ln_g, cf_ln_b, w_cf_out, w_o, g_ffn, peer_wq, peer_keys, peer_u, peer_v, g_ple, w_ple_gate, w_ple_proj, g_final):
    depth = w_in.shape[0]
    bp, lp, D = x_prompt.shape
    bs, ls, _ = x_sample.shape
    tp, ts = bp * lp, bs * ls
    T = tp + ts
    inner = SSD_HEADS * SSD_HEAD_DIM
    cd = state_ssd_conv.shape[-1]
    H, K = PEER_HEADS, PEER_KEYS
    tm, tb = _tiles(T)
    lc_p, lc_s = min(SSD_CHUNK, lp), min(SSD_CHUNK, ls)
    assert lp % lc_p == 0 and ls % lc_s == 0 and tp % lc_s == 0 and D == inner == cf_dw_w.shape[-1]

    h = jnp.concatenate([x_prompt.reshape(tp, D), x_sample.reshape(ts, D)], axis=0)
    expand = (jnp.arange(inner)[None, :] // SSD_HEAD_DIM == jnp.arange(LANES)[:, None]).astype(F32)
    zeros_conv = jnp.zeros((bp, SSD_CONV - 1, cd), F32)
    zeros_h = jnp.zeros((bp, SSD_HEADS, SSD_HEAD_DIM, SSD_STATE), F32)
    zeros_cf = jnp.zeros((bp, CF_CONV - 1, D), F32)
    row = lambda v: v.astype(F32)[None, :]
    eye_h = jnp.eye(H, dtype=F32)
    outs = {k: [] for k in ("ssd_p", "sconv_p", "cf_p", "ssd_s", "sconv_s", "cf_s")}

    for i in range(depth):
        w = w_in[i]
        o_xbc, o_dt = inner, inner + cd
        o_glu = o_dt + SSD_HEADS
        w_r = jnp.concatenate([w[:, :o_xbc], w[:, o_xbc:o_dt], w[:, o_glu:],
                               jnp.pad(w[:, o_dt:o_glu], ((0, 0), (0, LANES - SSD_HEADS)))], axis=1).astype(BF16)
        ncols = w_r.shape[1]
        tn = next(t for t in (ncols // 3, ncols) if t % LANES == 0 and ncols % t == 0)
        proj = _inproj(h, row(g_mix[i]), w_r, tm, tn)

        ssd_w = (ssd_conv_w[i], row(ssd_conv_b[i]), _pad_lanes(ssd_dt_bias[i]), _pad_lanes(ssd_a_log[i]),
                 row(jnp.repeat(ssd_d[i], SSD_HEAD_DIM)), row(ssd_norm_g[i]), expand, expand.T)
        ya_p, sconv_p, ssd_p = _ssd(proj, zeros_conv, zeros_h, ssd_w, nb=bp, nc=lp // lc_p, L=lc_p, row_off=0)
        ya_s, sconv_s, ssd_s = _ssd(proj, state_ssd_conv[i], state_ssd[i], ssd_w,
                                    nb=bs, nc=ls // lc_s, L=lc_s, row_off=tp)
        cf_w = (cf_dw_w[i], row(cf_dw_b[i]), row(cf_ln_g[i]), row(cf_ln_b[i]))
        yb_p, cf_p = _conformer(proj, zeros_cf, cf_w, nb=bp, nc=lp // lc_p, L=lc_p, row_off=0)
        yb_s, cf_s = _conformer(proj, state_cf_conv[i], cf_w, nb=bs, nc=ls // lc_s, L=lc_s, row_off=tp)
        for k, v in (("ssd_p", ssd_p), ("sconv_p", sconv_p), ("cf_p", cf_p),
                     ("ssd_s", ssd_s), ("sconv_s", sconv_s), ("cf_s", cf_s)):
            outs[k].append(v)
        ya = jnp.concatenate([ya_p, ya_s], axis=0)
        yb = jnp.concatenate([yb_p, yb_s], axis=0)
        h = _merge(h, ya, yb, proj, w_ssd_out[i].astype(BF16), w_cf_out[i].astype(BF16), w_o[i].astype(BF16), tm)

        qd = peer_keys.shape[-1]
        wq = peer_wq[i].reshape(D, H, 2, qd).transpose(0, 2, 1, 3).reshape(D, 2 * H * qd).astype(BF16)
        kb = [jnp.einsum("hkd,hg->khgd", peer_keys[i, :, s], eye_h).reshape(K * H, H * qd).astype(BF16)
              for s in range(2)]
        xnt, n_sel, e1, r2, e2 = _peer_select(h, row(g_ffn[i]), wq, kb[0], kb[1], tb)
        h = _peer_dense(h, xnt, n_sel, e1, r2, e2, _pack_table(peer_u[i], transpose=False), _pack_table(peer_v[i], transpose=True),
                        tb, ec=8 * K)

        p_all = jnp.concatenate([p_prompt[i].reshape(tp, -1), p_sample[i].reshape(ts, -1)], axis=0)
        h = _ple(h, p_all, row(g_ple[i]), w_ple_gate[i].astype(BF16), w_ple_proj[i].astype(BF16),
                 row(g_final), tm, final=(i == depth - 1))

    y_prompt = h[:tp].reshape(bp, lp, D)
    y_sample = h[tp:].reshape(bs, ls, D)
    st = {k: jnp.stack(v) for k, v in outs.items()}
    return (y_prompt, y_sample, st["ssd_p"], st["sconv_p"], st["cf_p"], st["ssd_s"], st["sconv_s"], st["cf_s"])
```

```python
import functools
import math

import jax
import jax.numpy as jnp
from jax import lax
from jax.experimental import pallas as pl
from jax.experimental.pallas import tpu as pltpu

F32, BF16 = jnp.float32, jnp.bfloat16
PACKED = jnp.uint32
EPS = 1e-6

SSD_HEADS, SSD_HEAD_DIM, SSD_GROUPS, SSD_STATE, SSD_CONV, SSD_CHUNK = 16, 64, 4, 128, 4, 128
CF_CONV = 31
PEER_HEADS, PEER_KEYS, PEER_TOPK = 8, 128, 16
LANES = 128
BF16_ROWS = 16
VMEM_LIMIT = 56 * 1024 * 1024

CAND_PAIRS = [(r1, r2) for r1 in range(PEER_TOPK) for r2 in range(PEER_TOPK)
              if (r1 + 1) * (r2 + 1) <= PEER_TOPK]


def _nt(a, b, **kw):
    return lax.dot_general(a, b, (((1,), (1,)), ((), ())), preferred_element_type=F32, **kw)


def _tn(a, b, **kw):
    return lax.dot_general(a, b, (((0,), (0,)), ((), ())), preferred_element_type=F32, **kw)


def _mm(a, b, **kw):
    return jnp.dot(a, b, preferred_element_type=F32, **kw)


def _rms(x, g):
    return x * lax.rsqrt(jnp.mean(x * x, axis=-1, keepdims=True) + EPS) * g


def _silu(x):
    return x * jax.nn.sigmoid(x)


def _pack_pair(lo, hi):
    bits = lambda v: lax.bitcast_convert_type(v.astype(BF16).astype(F32), PACKED)
    return (bits(hi) & jnp.uint32(0xFFFF0000)) | lax.shift_right_logical(bits(lo), jnp.uint32(16))


def _unpack_pair(packed, half):
    bits = (packed & jnp.uint32(0xFFFF0000)) if half else lax.shift_left(packed, jnp.uint32(16))
    return lax.bitcast_convert_type(bits, F32)


def _params(sem, flags=None):
    return pltpu.CompilerParams(dimension_semantics=sem, vmem_limit_bytes=VMEM_LIMIT, flags=flags)


def _inproj_kernel(x_ref, g_ref, w_ref, o_ref):
    o_ref[...] = _mm(_rms(x_ref[...], g_ref[...]).astype(BF16), w_ref[...])


def _inproj(h, g, w, tm, tn):
    T, D = h.shape
    N = w.shape[1]
    return pl.pallas_call(
        _inproj_kernel,
        grid=(N // tn, T // tm),
        in_specs=[pl.BlockSpec((tm, D), lambda j, i: (i, 0)),
                  pl.BlockSpec((1, D), lambda j, i: (0, 0)),
                  pl.BlockSpec((D, tn), lambda j, i: (0, j))],
        out_specs=pl.BlockSpec((tm, tn), lambda j, i: (i, j)),
        out_shape=jax.ShapeDtypeStruct((T, N), F32),
        compiler_params=_params(("parallel", "parallel")),
        name="inproj",
    )(h, g, w)


def _split3(x):
    x1 = x.astype(BF16)
    r1 = x - x1.astype(F32)
    x2 = r1.astype(BF16)
    x3 = (r1 - x2.astype(F32)).astype(BF16)
    return x1, x2, x3


def _mm_sel(x, sel):
    x1, x2, x3 = _split3(x)
    return (_mm(x3, sel) + _mm(x2, sel)) + _mm(x1, sel)


def _sel_mm(sel, x):
    x1, x2, x3 = _split3(x)
    return (_mm(sel, x3) + _mm(sel, x2)) + _mm(sel, x1)


def _sel_nt(sel, x):
    x1, x2, x3 = _split3(x)
    return (_nt(sel, x3) + _nt(sel, x2)) + _nt(sel, x1)


def _ssd_kernel(z_ref, xs_ref, bc_ref, dt_ref, conv0_ref, h0_ref, cw_ref, cb_ref, dtb_ref, alog_ref,
                dexp_ref, ng_ref, e_ref, et_ref,
                y_ref, nconv_ref, ht_ref,
                xpad_ref, st_ref, yscr_ref, *, L, nseq):
    c = pl.program_id(1)
    inner = SSD_HEADS * SSD_HEAD_DIM
    hist = SSD_CONV - 1
    base = 8
    gn = SSD_GROUPS * SSD_STATE
    gw = inner // SSD_GROUPS
    heads_per_group = SSD_HEADS // SSD_GROUPS

    @pl.when(c == 0)
    def _():
        xpad_ref[:, base - hist:base, :] = conv0_ref[...]
        st_ref[...] = h0_ref[...]

    row = lax.broadcasted_iota(jnp.int32, (L, L), 0)
    col = lax.broadcasted_iota(jnp.int32, (L, L), 1)
    causal = row >= col
    tri = jnp.where(causal, 1.0, 0.0).astype(BF16)
    lane = lax.broadcasted_iota(jnp.int32, (L, 2 * SSD_HEAD_DIM), 1)
    a_neg = -jnp.exp(alog_ref[...])
    e = e_ref[...]

    for q in range(nseq):
        rows = slice(q * L, (q + 1) * L)
        xpad_ref[q, base:base + L, 0:inner] = xs_ref[rows, :]
        xpad_ref[q, base:base + L, inner:] = bc_ref[rows, :]

        conv = cb_ref[...] + cw_ref[0:1, :] * xpad_ref[q, base - hist:base - hist + L, :]
        for k in range(1, SSD_CONV):
            conv = conv + cw_ref[k:k + 1, :] * xpad_ref[q, base - hist + k:base - hist + k + L, :]
        last_rows = xpad_ref[q, base + L - hist:base + L, :]
        nconv_ref[q] = last_rows
        xpad_ref[q, base - hist:base, :] = last_rows
        xc = _silu(conv)
        xs = xc[:, :inner]
        bm = xc[:, inner:inner + gn].astype(BF16)
        cm = xc[:, inner + gn:].astype(BF16)

        x = dt_ref[rows, :] + dtb_ref[...]
        dt = jnp.maximum(x, 0.0) + jnp.log1p(jnp.exp(-jnp.abs(x)))
        acs = _sel_mm(tri, dt * a_neg)
        dt_x = _mm_sel(dt, e)
        acs_x = _mm_sel(acs, e)
        acs_t = _sel_nt(et_ref[...], acs)
        alast_x = acs_x[L - 1:L, :]
        xd = xs * dt_x
        xd_bf = xd.astype(BF16)
        xdec = (xd * jnp.exp(alast_x - acs_x)).astype(BF16)
        eacs = jnp.exp(acs_x)

        for g in range(SSD_GROUPS):
            bg = bm[:, g * SSD_STATE:(g + 1) * SSD_STATE]
            cg = cm[:, g * SSD_STATE:(g + 1) * SSD_STATE]
            cb = _nt(cg, bg)
            for pp in range(heads_per_group // 2):
                p = g * (heads_per_group // 2) + pp
                cols = slice(p * 2 * SSD_HEAD_DIM, (p + 1) * 2 * SSD_HEAD_DIM)
                xp = xd_bf[:, cols]
                ys = []
                for hh in range(2):
                    h = 2 * p + hh
                    acol = acs[:, h:h + 1]
                    arow = acs_t[h * SSD_HEAD_DIM:h * SSD_HEAD_DIM + 1, :]
                    lm = jnp.where(causal, jnp.exp(acol - arow), 0.0)
                    ys.append(_mm((cb * lm).astype(BF16), xp))
                ydiag = jnp.where(lane < SSD_HEAD_DIM, ys[0], ys[1])
                st = st_ref[q, p]
                yoff = _nt(cg, st.astype(BF16)) * eacs[:, cols]
                yscr_ref[rows, cols] = ydiag + yoff
                scale = jnp.exp(acs_t[p * 2 * SSD_HEAD_DIM:(p + 1) * 2 * SSD_HEAD_DIM, L - 1:L])
                st_ref[q, p] = st * scale + _tn(xdec[:, cols], bg)

        y = (yscr_ref[rows, :] + dexp_ref[...] * xs) * _silu(z_ref[rows, :])
        for g in range(SSD_GROUPS):
            seg = y[:, g * gw:(g + 1) * gw]
            seg = seg * lax.rsqrt(jnp.mean(seg * seg, axis=-1, keepdims=True) + EPS)
            y_ref[rows, g * gw:(g + 1) * gw] = seg * ng_ref[:, g * gw:(g + 1) * gw]

    ht_ref[...] = st_ref[...]


def _ssd(proj, conv0, h0, wts, *, nb, nc, L, row_off, nseq):
    assert nb % nseq == 0 and (nseq == 1 or nc == 1) and row_off % (nseq * L) == 0
    inner = SSD_HEADS * SSD_HEAD_DIM
    cd = conv0.shape[-1]
    R = nseq * L
    r0 = row_off // R
    rows = lambda b, c: r0 + b * nc + c
    dt_blk = (proj.shape[1] - LANES) // LANES
    const = lambda shape: pl.BlockSpec(shape, lambda b, c: (0,) * len(shape))
    npair = SSD_HEADS // 2
    y, nconv, ht = pl.pallas_call(
        functools.partial(_ssd_kernel, L=L, nseq=nseq),
        grid=(nb // nseq, nc),
        in_specs=[pl.BlockSpec((R, inner), lambda b, c: (rows(b, c), 0)),
                  pl.BlockSpec((R, inner), lambda b, c: (rows(b, c), 1)),
                  pl.BlockSpec((R, inner), lambda b, c: (rows(b, c), 2)),
                  pl.BlockSpec((R, LANES), lambda b, c: (rows(b, c), dt_blk)),
                  pl.BlockSpec((nseq, SSD_CONV - 1, cd), lambda b, c: (b, 0, 0)),
                  pl.BlockSpec((nseq, npair, LANES, SSD_STATE), lambda b, c: (b, 0, 0, 0)),
                  const((SSD_CONV, cd)), const((1, cd)), const((1, LANES)), const((1, LANES)),
                  const((1, inner)), const((1, inner)), const((LANES, inner)), const((inner, LANES))],
        out_specs=[pl.BlockSpec((R, inner), lambda b, c: (b * nc + c, 0)),
                   pl.BlockSpec((nseq, SSD_CONV - 1, cd), lambda b, c: (b, 0, 0)),
                   pl.BlockSpec((nseq, npair, LANES, SSD_STATE), lambda b, c: (b, 0, 0, 0))],
        out_shape=[jax.ShapeDtypeStruct((nb * nc * L, inner), F32),
                   jax.ShapeDtypeStruct((nb, SSD_CONV - 1, cd), F32),
                   jax.ShapeDtypeStruct((nb, npair, LANES, SSD_STATE), F32)],
        scratch_shapes=[pltpu.VMEM((nseq, 8 + L, cd), F32),
                        pltpu.VMEM((nseq, npair, LANES, SSD_STATE), F32),
                        pltpu.VMEM((R, inner), F32)],
        compiler_params=_params(("parallel", "arbitrary")),
        name=f"ssd_L{L}",
    )(proj, proj, proj, proj, conv0, h0.reshape(nb, npair, LANES, SSD_STATE), *wts)
    return y, nconv, ht.reshape(nb, SSD_HEADS, SSD_HEAD_DIM, SSD_STATE)


def _cf_kernel(a_ref, g_ref, buf0_ref, w_ref, b_ref, lng_ref, lnb_ref, y_ref, nbuf_ref,
               upad_ref, shift_ref, conv_ref, *, L):
    c = pl.program_id(1)
    hist = CF_CONV - 1
    base = 32
    dim = a_ref.shape[1]

    @pl.when(c == 0)
    def _():
        upad_ref[base - hist:base, :] = buf0_ref[0]

    upad_ref[base:base + L, :] = a_ref[...] * jax.nn.sigmoid(g_ref[...])
    first = base - hist
    span = shift_ref.shape[1]
    slab = 2 * LANES
    for s in range(dim // slab):
        cols = slice(s * slab, (s + 1) * slab)
        acc = jnp.broadcast_to(b_ref[:, cols], (L, slab))
        for phase in range(8):
            if phase:
                shift_ref[phase - 1, :, cols] = upad_ref[phase:phase + span, cols]
            for k in range(CF_CONV):
                if (first + k) % 8 == phase:
                    r0 = first + k - phase
                    src = upad_ref[r0:r0 + L, cols] if phase == 0 else shift_ref[phase - 1, r0:r0 + L, cols]
                    acc = acc + w_ref[k:k + 1, cols] * src
        conv_ref[:, cols] = acc
    nbuf = upad_ref[base + L - hist:base + L, :]
    nbuf_ref[0] = nbuf
    upad_ref[base - hist:base, :] = nbuf

    x = conv_ref[...]
    xc = x - jnp.mean(x, axis=-1, keepdims=True)
    y = xc * lax.rsqrt(jnp.mean(xc * xc, axis=-1, keepdims=True) + EPS) * lng_ref[...] + lnb_ref[...]
    y_ref[...] = _silu(y)


def _conformer(proj, buf0, wts, *, nb, nc, L, row_off):
    dim = buf0.shape[-1]
    r0 = row_off // L
    rows = lambda b, c: r0 + b * nc + c
    const = lambda shape: pl.BlockSpec(shape, lambda b, c: (0,) * len(shape))
    return pl.pallas_call(
        functools.partial(_cf_kernel, L=L),
        grid=(nb, nc),
        in_specs=[pl.BlockSpec((L, dim), lambda b, c: (rows(b, c), 3)),
                  pl.BlockSpec((L, dim), lambda b, c: (rows(b, c), 4)),
                  pl.BlockSpec((1, CF_CONV - 1, dim), lambda b, c: (b, 0, 0)),
                  const((CF_CONV, dim)), const((1, dim)), const((1, dim)), const((1, dim))],
        out_specs=[pl.BlockSpec((L, dim), lambda b, c: (b * nc + c, 0)),
                   pl.BlockSpec((1, CF_CONV - 1, dim), lambda b, c: (b, 0, 0))],
        out_shape=[jax.ShapeDtypeStruct((nb * nc * L, dim), F32),
                   jax.ShapeDtypeStruct((nb, CF_CONV - 1, dim), F32)],
        scratch_shapes=[pltpu.VMEM((32 + L, dim), F32), pltpu.VMEM((7, 24 + L, dim), F32),
                        pltpu.VMEM((L, dim), F32)],
        compiler_params=_params(("parallel", "arbitrary")),
        name=f"conformer_L{L}",
    )(proj, proj, buf0, *wts)


def _merge_kernel(h_ref, ya_ref, yb_ref, ga_ref, gb_ref, wa_ref, wb_ref, wo_ref, o_ref):
    mix = (jax.nn.sigmoid(ga_ref[...]) * _mm(ya_ref[...].astype(BF16), wa_ref[...])
           + jax.nn.sigmoid(gb_ref[...]) * _mm(yb_ref[...].astype(BF16), wb_ref[...]))
    o_ref[...] = h_ref[...] + _mm(mix.astype(BF16), wo_ref[...])


def _merge(h, ya, yb, proj, wa, wb, wo, tm):
    T, D = h.shape
    rowblk = lambda j: pl.BlockSpec((tm, D), lambda i: (i, j))
    wspec = pl.BlockSpec((D, D), lambda i: (0, 0))
    return pl.pallas_call(
        _merge_kernel,
        grid=(T // tm,),
        in_specs=[rowblk(0), rowblk(0), rowblk(0), rowblk(5), rowblk(6), wspec, wspec, wspec],
        out_specs=rowblk(0),
        out_shape=jax.ShapeDtypeStruct((T, D), F32),
        compiler_params=_params(("parallel",)),
        name="merge",
    )(h, ya, yb, proj, proj, wa, wb, wo)


def _extract_topk(s_ref, rank_ref, vals_ref):
    rank_ref[...] = jnp.full(rank_ref.shape, float(PEER_TOPK), F32)
    n = s_ref.shape[0]

    def step(k, carry):
        s = s_ref[...]
        m = jnp.max(s, axis=0)
        idx = lax.broadcasted_iota(jnp.int32, s.shape, 0)
        first = jnp.min(jnp.where(s == m[None], idx, n), axis=0)
        hit = idx == first[None]
        rank_ref[...] = jnp.where(hit, jnp.asarray(k, F32), rank_ref[...])
        s_ref[...] = jnp.where(hit, -jnp.inf, s)
        vals_ref[k] = m
        return carry

    lax.fori_loop(0, PEER_TOPK, step, 0)


def _sort16_desc(x):
    x = list(x)
    n = len(x)
    k = 2
    while k <= n:
        j = k // 2
        while j >= 1:
            for i in range(n):
                l = i ^ j
                if l > i:
                    hi, lo = jnp.maximum(x[i], x[l]), jnp.minimum(x[i], x[l])
                    x[i], x[l] = (hi, lo) if (i & k) == 0 else (lo, hi)
            j //= 2
        k *= 2
    return x


def _merge_top16(a, b):
    n = len(a)
    x = [jnp.maximum(a[i], b[n - 1 - i]) for i in range(n)]
    j = n // 2
    while j >= 1:
        for i in range(n):
            l = i ^ j
            if l > i:
                x[i], x[l] = jnp.maximum(x[i], x[l]), jnp.minimum(x[i], x[l])
        j //= 2
    return x


def _top16_sorted(rows):
    groups = [_sort16_desc(rows[g:g + PEER_TOPK]) for g in range(0, len(rows), PEER_TOPK)]
    while len(groups) > 1:
        groups = [_merge_top16(groups[i], groups[i + 1]) for i in range(0, len(groups), 2)]
    return groups[0]


def _select_kernel(h_ref, g_ref, wq_ref, kb1_ref, kb2_ref,
                   xnt_ref, n_ref, e1_ref, r2_ref, e2_ref,
                   s1_scr, s2_scr, w_scr, rk1_scr, rk2_scr, v1_scr, v2_scr,
                   cw_scr, crk_scr, cv_scr, pk_scr, *, tb):
    H, K = PEER_HEADS, PEER_KEYS
    half = H * K
    xn = _rms(h_ref[...], g_ref[...])
    xnt_ref[...] = xn.T.astype(BF16)
    q = _mm(xn.astype(BF16), wq_ref[...]).astype(BF16)
    s1 = _nt(kb1_ref[...], q[:, :half])
    s2 = _nt(kb2_ref[...], q[:, half:])
    nlt = tb // LANES
    for lt in range(nlt):
        s1_scr[lt] = s1[:, lt * LANES:(lt + 1) * LANES].reshape(K, H, LANES)
        s2_scr[lt] = s2[:, lt * LANES:(lt + 1) * LANES].reshape(K, H, LANES)

    topk = float(PEER_TOPK)

    def by_value(lt):
        a = _top16_sorted([s1_scr[lt, k] for k in range(K)])
        b = _top16_sorted([s2_scr[lt, k] for k in range(K)])
        cands = [a[r1] + b[r2] for r1, r2 in CAND_PAIRS]
        pad = [jnp.full((H, LANES), -jnp.inf, F32)] * (-len(cands) % PEER_TOPK)
        t = _top16_sorted(cands + pad)
        won = [jnp.where(c >= t[-1], 1.0, 0.0) for c in cands]
        ncnt = [sum(w for w, (r1, _) in zip(won, CAND_PAIRS) if r1 == r) for r in range(PEER_TOPK)]
        tie = jnp.where(sum(won) != topk, 1.0, 0.0)
        for v in (a, b, t):
            for k in range(PEER_TOPK - 1):
                tie = jnp.where(v[k] == v[k + 1], 1.0, tie)
        rz = 1.0 / sum(jnp.exp(v - t[0]) for v in t)

        def keys0(i, cnt):
            s = s1_scr[lt, i]
            n_i = jnp.zeros_like(s)
            for r in range(PEER_TOPK):
                n_i = jnp.where(s == a[r], ncnt[r], n_i)
            rows = pl.ds(pl.multiple_of(i * H, H), H)
            pk_scr[0, rows, :] = n_i
            pk_scr[1, rows, :] = jnp.exp(s - a[0]) * rz
            return cnt + jnp.where(s >= a[-1], 1.0, 0.0)

        def keys1(j, cnt):
            s = s2_scr[lt, j]
            rank = jnp.full_like(s, topk)
            for r in range(PEER_TOPK):
                rank = jnp.where(s == b[r], float(r), rank)
            rows = pl.ds(pl.multiple_of(j * H, H), H)
            pk_scr[2, rows, :] = rank
            pk_scr[3, rows, :] = jnp.exp(s - b[0])
            return cnt + jnp.where(s >= b[-1], 1.0, 0.0)

        zero = jnp.zeros((H, LANES), F32)
        for body in (keys0, keys1):
            tie = jnp.where(lax.fori_loop(0, K, body, zero, unroll=8) != topk, 1.0, tie)
        return tie

    def by_extraction(lt):
        w_scr[...] = s1_scr[lt]
        _extract_topk(w_scr, rk1_scr, v1_scr)
        w_scr[...] = s2_scr[lt]
        _extract_topk(w_scr, rk2_scr, v2_scr)
        for p, (r1, r2) in enumerate(CAND_PAIRS):
            cw_scr[p] = v1_scr[r1] + v2_scr[r2]
        _extract_topk(cw_scr, crk_scr, cv_scr)
        top = cv_scr[0]
        zsum = jnp.zeros_like(top)
        for k in range(PEER_TOPK):
            zsum = zsum + jnp.exp(cv_scr[k] - top)
        rz = 1.0 / zsum
        rk1 = rk1_scr[...]
        n_i = jnp.zeros_like(rk1)
        for r1 in range(PEER_TOPK):
            cnt = jnp.zeros_like(top)
            for p, (a, _) in enumerate(CAND_PAIRS):
                if a == r1:
                    cnt = cnt + jnp.where(crk_scr[p] < topk, 1.0, 0.0)
            n_i = jnp.where(rk1 == float(r1), cnt[None], n_i)
        pk_scr[0] = n_i.reshape(half, LANES)
        pk_scr[1] = (jnp.exp(s1_scr[lt] - v1_scr[0][None]) * rz[None]).reshape(half, LANES)
        pk_scr[2] = rk2_scr[...].reshape(half, LANES)
        pk_scr[3] = jnp.exp(s2_scr[lt] - v2_scr[0][None]).reshape(half, LANES)

    def lane_tile(lt, carry):
        tie = by_value(lt)

        @pl.when(jnp.max(tie) > 0.0)
        def _():
            by_extraction(lt)

        lanes = pl.ds(pl.multiple_of(lt * LANES, LANES), LANES)
        for h in range(H):
            n_ref[h, :, lanes] = pk_scr[0, pl.ds(h, K, stride=H), :]
            e1_ref[h, :, lanes] = pk_scr[1, pl.ds(h, K, stride=H), :]
            for a, o_ref in ((2, r2_ref), (3, e2_ref)):
                lo = pk_scr[a, pl.ds(h, K // 2, stride=H), :]
                hi = pk_scr[a, pl.ds(h + (K // 2) * H, K // 2, stride=H), :]
                packed = _pack_pair(lo, hi)
                for jt in range(K // BF16_ROWS):
                    o_ref[lt, jt, h] = packed[jt * 8:(jt + 1) * 8]
        return carry

    lax.fori_loop(0, nlt, lane_tile, 0)


def _peer_select(h1, g, wq, kb1, kb2, tb):
    T, D = h1.shape
    H, K = PEER_HEADS, PEER_KEYS
    nlt = tb // LANES
    ncand = len(CAND_PAIRS)
    const = lambda shape: pl.BlockSpec(shape, lambda i: (0,) * len(shape))
    hkt = jax.ShapeDtypeStruct((H, K, T), F32)
    hkt_spec = pl.BlockSpec((H, K, tb), lambda i: (0, 0, i))
    pair = jax.ShapeDtypeStruct((T // LANES, K // BF16_ROWS, H, 8, LANES), PACKED)
    pair_spec = pl.BlockSpec((nlt, K // BF16_ROWS, H, 8, LANES), lambda i: (i, 0, 0, 0, 0))
    tile = lambda n: pltpu.VMEM((n, H, LANES), F32)
    return pl.pallas_call(
        functools.partial(_select_kernel, tb=tb),
        grid=(T // tb,),
        in_specs=[pl.BlockSpec((tb, D), lambda i: (i, 0)), const((1, D)), const(wq.shape),
                  const(kb1.shape), const(kb2.shape)],
        out_specs=[pl.BlockSpec((D, tb), lambda i: (0, i)), hkt_spec, hkt_spec, pair_spec, pair_spec],
        out_shape=[jax.ShapeDtypeStruct((D, T), BF16), hkt, hkt, pair, pair],
        scratch_shapes=[pltpu.VMEM((nlt, K, H, LANES), F32), pltpu.VMEM((nlt, K, H, LANES), F32),
                        tile(K), tile(K), tile(K), tile(PEER_TOPK), tile(PEER_TOPK),
                        tile(ncand), tile(ncand), tile(PEER_TOPK),
                        pltpu.VMEM((4, H * K, LANES), F32)],
        compiler_params=_params(("parallel",)),
        name="peer_select",
    )(h1, g, wq, kb1, kb2)


def _gelu_tanh(x):
    c = math.sqrt(2.0 / math.pi)
    hx = 0.5 * x
    return hx * jnp.tanh(x * (c + (c * 0.044715) * (x * x))) + hx


def _dense_kernel(h_ref, xnt_ref, n_ref, e1_ref, r2_ref, e2_ref, u_ref, vt_ref, o_ref,
                  pre0_scr, pre1_scr, act0_scr, act1_scr, acc_scr, *, tb, ec):
    k = pl.program_id(1)
    last = pl.num_programs(1) - 1
    H, K = PEER_HEADS, PEER_KEYS
    ni = ec // K
    zero = jnp.zeros((), BF16)
    lw = 2 * LANES

    def stage_b(cb, pre_ref, act_ref, lp):
        for lt in range(lw // LANES):
            ltile = lp * (lw // LANES) + lt
            lanes = pl.ds(pl.multiple_of(ltile * LANES, LANES), LANES)
            for i8 in range(ni // 8):
                irows = pl.ds(pl.multiple_of(cb * ni + i8 * 8, 8), 8)
                n8 = [n_ref[h, irows, lanes] for h in range(H)]
                e8 = [e1_ref[h, irows, lanes] for h in range(H)]
                for i1 in range(8):
                    il = i8 * 8 + i1
                    nb = [jnp.broadcast_to(n8[h][i1:i1 + 1, :], (BF16_ROWS, LANES)).astype(BF16) for h in range(H)]
                    eb = [jnp.broadcast_to(e8[h][i1:i1 + 1, :], (BF16_ROWS, LANES)).astype(BF16) for h in range(H)]
                    for jt in range(K // BF16_ROWS):
                        gate = None
                        for h in range(H):
                            r = pltpu.bitcast(r2_ref[ltile, jt, h], BF16)
                            e = pltpu.bitcast(e2_ref[ltile, jt, h], BF16)
                            t = jnp.where(r < nb[h], e, zero) * eb[h]
                            gate = t if gate is None else gate + t
                        gate = pltpu.bitcast(gate, PACKED)
                        for half in range(2):
                            g = _unpack_pair(gate, half)
                            r0 = il * K + half * (K // 2) + jt * 8
                            act_ref[r0:r0 + 8, lanes] = (_gelu_tanh(pre_ref[r0:r0 + 8, lanes]) * g).astype(BF16)

    def sub_step(sub, do_a, do_b, do_c):
        s = 2 * k + sub
        first = s == 2
        pre_w, pre_r = (pre0_scr, pre1_scr) if sub == 0 else (pre1_scr, pre0_scr)
        act_r, act_w = (act0_scr, act1_scr) if sub == 0 else (act1_scr, act0_scr)

        def body(lp, carry):
            lanes = pl.ds(pl.multiple_of(lp * lw, lw), lw)
            if do_a:
                u = pltpu.bitcast(u_ref[sub * (ec // 2):(sub + 1) * (ec // 2), :], BF16)
                pre_w[:, lanes] = _mm(u, xnt_ref[:, lanes])
            if do_b:
                stage_b(s - 1, pre_r, act_w, lp)
            if do_c:
                vt = pltpu.bitcast(vt_ref[:, sub * ec:(sub + 1) * ec], BF16)
                contrib = _mm(vt, act_r[:, lanes])
                acc_scr[:, lanes] = jnp.where(first, contrib, acc_scr[:, lanes] + contrib)
            return carry

        lax.fori_loop(0, tb // lw, body, 0)

    @pl.when(k == 0)
    def _():
        sub_step(0, True, False, False)
        sub_step(1, True, True, False)

    @pl.when((k > 0) & (k < last))
    def _():
        sub_step(0, True, True, True)
        sub_step(1, True, True, True)

    @pl.when(k == last)
    def _():
        sub_step(0, False, True, True)
        sub_step(1, False, False, True)

    @pl.when(k == last)
    def _():
        o_ref[...] = h_ref[...] + acc_scr[...].T


def _pack_kernel(w_ref, o_ref, *, transpose):
    w = w_ref[...]
    o_ref[...] = pltpu.bitcast((w.T if transpose else w).astype(BF16), PACKED)


def _pack_table(w, *, transpose, rows=1024):
    E, D = w.shape
    if transpose:
        out_shape, out_spec = (D // 2, E), pl.BlockSpec((D // 2, rows), lambda i: (0, i))
    else:
        out_shape, out_spec = (E // 2, D), pl.BlockSpec((rows // 2, D), lambda i: (i, 0))
    return pl.pallas_call(
        functools.partial(_pack_kernel, transpose=transpose),
        grid=(E // rows,),
        in_specs=[pl.BlockSpec((rows, D), lambda i: (i, 0))],
        out_specs=out_spec,
        out_shape=jax.ShapeDtypeStruct(out_shape, PACKED),
        compiler_params=_params(("parallel",)),
        name="pack_table_t" if transpose else "pack_table",
    )(w)


def _peer_dense(h1, xnt, n, e1, r2, e2, u, vt, tb, ec):
    T, D = h1.shape
    H, K = PEER_HEADS, PEER_KEYS
    nc = 2 * u.shape[0] // ec
    assert nc % 2 == 0 and nc >= 4
    nsteps = nc // 2 + 1
    hkt_spec = pl.BlockSpec((H, K, tb), lambda i, k: (0, 0, i))
    pair_spec = pl.BlockSpec((tb // LANES, K // BF16_ROWS, H, 8, LANES), lambda i, k: (i, 0, 0, 0, 0))
    return pl.pallas_call(
        functools.partial(_dense_kernel, tb=tb, ec=ec),
        grid=(T // tb, nsteps),
        in_specs=[pl.BlockSpec((tb, D), lambda i, k: (i, 0)),
                  pl.BlockSpec((D, tb), lambda i, k: (0, i)),
                  hkt_spec, hkt_spec, pair_spec, pair_spec,
                  pl.BlockSpec((ec, D), lambda i, k: (jnp.minimum(k, nsteps - 2), 0)),
                  pl.BlockSpec((D // 2, 2 * ec), lambda i, k: (0, jnp.maximum(k - 1, 0)))],
        out_specs=pl.BlockSpec((tb, D), lambda i, k: (i, 0)),
        out_shape=jax.ShapeDtypeStruct((T, D), F32),
        scratch_shapes=[pltpu.VMEM((ec, tb), F32), pltpu.VMEM((ec, tb), F32),
                        pltpu.VMEM((ec, tb), BF16), pltpu.VMEM((ec, tb), BF16), pltpu.VMEM((D, tb), F32)],
        compiler_params=_params(("arbitrary", "arbitrary")),
        name="peer_dense",
    )(h1, xnt, n, e1, r2, e2, u, vt)


def _ple_kernel(h_ref, p_ref, g_ref, wg_ref, wp_ref, gf_ref, *o_refs, split):
    h = h_ref[...]
    gate = jax.nn.sigmoid(_mm(_rms(h, g_ref[...]).astype(BF16), wg_ref[...]))
    h = h + gate * _mm(p_ref[...].astype(BF16), wp_ref[...])
    if split is None:
        o_refs[0][...] = h
        return
    y = _rms(h, gf_ref[...])
    o_refs[1][...] = y

    @pl.when(pl.program_id(0) < split)
    def _():
        o_refs[0][...] = y


def _ple(h, p, g, wg, wp, gf, tm, split_rows=None):
    T, D = h.shape
    P = p.shape[1]
    const = lambda shape: pl.BlockSpec(shape, lambda i: (0,) * len(shape))
    tile = lambda fn: pl.BlockSpec((tm, D), fn)
    if split_rows is None:
        split, out_specs, out_shape = None, tile(lambda i: (i, 0)), jax.ShapeDtypeStruct((T, D), F32)
    else:
        split = split_rows // tm
        out_specs = [tile(lambda i: (jnp.minimum(i, split - 1), 0)), tile(lambda i: (jnp.maximum(i - split, 0), 0))]
        out_shape = [jax.ShapeDtypeStruct((split_rows, D), F32), jax.ShapeDtypeStruct((T - split_rows, D), F32)]
    return pl.pallas_call(
        functools.partial(_ple_kernel, split=split),
        grid=(T // tm,),
        in_specs=[tile(lambda i: (i, 0)), pl.BlockSpec((tm, P), lambda i: (i, 0)),
                  const((1, D)), const((D, D)), const((P, D)), const((1, D))],
        out_specs=out_specs,
        out_shape=out_shape,
        compiler_params=_params(("arbitrary",)),
        name="ple",
    )(h, p, g, wg, wp, gf)


def _tiles(tp, ts):
    tm = next(t for t in (512, 256, 128, 64, 32, 16, 8) if tp % t == 0 and ts % t == 0)
    tb = next(t for t in (512, 256) if (tp + ts) % t == 0)
    return tm, tb


def _pad_lanes(v, fill=0.0):
    return jnp.pad(v.astype(F32), (0, LANES - v.shape[0]), constant_values=fill)[None, :]


def kernel(x_prompt, x_sample, state_ssd, state_ssd_conv, state_cf_conv, p_prompt, p_sample, g_mix, w_in, ssd_conv_w, ssd_conv_b, ssd_dt_bias, ssd_a_log, ssd_d, ssd_norm_g, w_ssd_out, cf_dw_w, cf_dw_b, cf_ln_g, cf_ln_b, w_cf_out, w_o, g_ffn, peer_wq, peer_keys, peer_u, peer_v, g_ple, w_ple_gate, w_ple_proj, g_final):
    depth = w_in.shape[0]
    bp, lp, D = x_prompt.shape
    bs, ls, _ = x_sample.shape
    tp, ts = bp * lp, bs * ls
    T = tp + ts
    inner = SSD_HEADS * SSD_HEAD_DIM
    cd = state_ssd_conv.shape[-1]
    H, K = PEER_HEADS, PEER_KEYS
    tm, tb = _tiles(tp, ts)
    lc_p, lc_s = min(SSD_CHUNK, lp), min(SSD_CHUNK, ls)
    assert lp % lc_p == 0 and ls % lc_s == 0 and tp % lc_s == 0 and D == inner == cf_dw_w.shape[-1]

    h = jnp.concatenate([x_prompt.reshape(tp, D), x_sample.reshape(ts, D)], axis=0)
    expand = (jnp.arange(inner)[None, :] // SSD_HEAD_DIM == jnp.arange(LANES)[:, None]).astype(BF16)
    seqs_per_step = next(n for n in (8, 4, 2, 1) if bs % n == 0 and tp % (n * lc_s) == 0) if ls == lc_s else 1
    zeros_conv = jnp.zeros((bp, SSD_CONV - 1, cd), F32)
    zeros_h = jnp.zeros((bp, SSD_HEADS, SSD_HEAD_DIM, SSD_STATE), F32)
    zeros_cf = jnp.zeros((bp, CF_CONV - 1, D), F32)
    row = lambda v: v.astype(F32)[None, :]
    eye_h = jnp.eye(H, dtype=F32)
    outs = {k: [] for k in ("ssd_p", "sconv_p", "cf_p", "ssd_s", "sconv_s", "cf_s")}

    for i in range(depth):
        w = w_in[i]
        o_xbc, o_dt = inner, inner + cd
        o_glu = o_dt + SSD_HEADS
        w_r = jnp.concatenate([w[:, :o_xbc], w[:, o_xbc:o_dt], w[:, o_glu:],
                               jnp.pad(w[:, o_dt:o_glu], ((0, 0), (0, LANES - SSD_HEADS)))], axis=1).astype(BF16)
        ncols = w_r.shape[1]
        tn = next(t for t in (ncols // 3, ncols) if t % LANES == 0 and ncols % t == 0)
        proj = _inproj(h, row(g_mix[i]), w_r, tm, tn)

        ssd_w = (ssd_conv_w[i], row(ssd_conv_b[i]), _pad_lanes(ssd_dt_bias[i]), _pad_lanes(ssd_a_log[i]),
                 row(jnp.repeat(ssd_d[i], SSD_HEAD_DIM)), row(ssd_norm_g[i]), expand, expand.T)
        ya_p, sconv_p, ssd_p = _ssd(proj, zeros_conv, zeros_h, ssd_w, nb=bp, nc=lp // lc_p, L=lc_p, row_off=0,
                                    nseq=1)
        ya_s, sconv_s, ssd_s = _ssd(proj, state_ssd_conv[i], state_ssd[i], ssd_w,
                                    nb=bs, nc=ls // lc_s, L=lc_s, row_off=tp, nseq=seqs_per_step)
        cf_w = (cf_dw_w[i], row(cf_dw_b[i]), row(cf_ln_g[i]), row(cf_ln_b[i]))
        yb_p, cf_p = _conformer(proj, zeros_cf, cf_w, nb=bp, nc=lp // lc_p, L=lc_p, row_off=0)
        yb_s, cf_s = _conformer(proj, state_cf_conv[i], cf_w, nb=bs, nc=ls // lc_s, L=lc_s, row_off=tp)
        for k, v in (("ssd_p", ssd_p), ("sconv_p", sconv_p), ("cf_p", cf_p),
                     ("ssd_s", ssd_s), ("sconv_s", sconv_s), ("cf_s", cf_s)):
            outs[k].append(v)
        ya = jnp.concatenate([ya_p, ya_s], axis=0)
        yb = jnp.concatenate([yb_p, yb_s], axis=0)
        h = _merge(h, ya, yb, proj, w_ssd_out[i].astype(BF16), w_cf_out[i].astype(BF16), w_o[i].astype(BF16), tm)

        qd = peer_keys.shape[-1]
        wq = peer_wq[i].reshape(D, H, 2, qd).transpose(0, 2, 1, 3).reshape(D, 2 * H * qd).astype(BF16)
        kb = [jnp.einsum("hkd,hg->khgd", peer_keys[i, :, s], eye_h).reshape(K * H, H * qd).astype(BF16)
              for s in range(2)]
        xnt, n_sel, e1, r2, e2 = _peer_select(h, row(g_ffn[i]), wq, kb[0], kb[1], tb)
        h = _peer_dense(h, xnt, n_sel, e1, r2, e2, _pack_table(peer_u[i], transpose=False), _pack_table(peer_v[i], transpose=True),
                        tb, ec=8 * K)

        p_all = jnp.concatenate([p_prompt[i].reshape(tp, -1), p_sample[i].reshape(ts, -1)], axis=0)
        h = _ple(h, p_all, row(g_ple[i]), w_ple_gate[i].astype(BF16), w_ple_proj[i].astype(BF16),
                 row(g_final), tm, split_rows=tp if i == depth - 1 else None)

    y_prompt, y_sample = h
    st = {k: jnp.stack(v) for k, v in outs.items()}
    return (y_prompt.reshape(bp, lp, D), y_sample.reshape(bs, ls, D),
            st["ssd_p"], st["sconv_p"], st["cf_p"], st["ssd_s"], st["sconv_s"], st["cf_s"])
```

```python
import functools
import math

import jax
import jax.numpy as jnp
from jax import lax
from jax.experimental import pallas as pl
from jax.experimental.pallas import tpu as pltpu

F32, BF16 = jnp.float32, jnp.bfloat16
PACKED = jnp.uint32
EPS = 1e-6

SSD_HEADS, SSD_HEAD_DIM, SSD_GROUPS, SSD_STATE, SSD_CONV, SSD_CHUNK = 16, 64, 4, 128, 4, 128
CF_CONV = 31
PEER_HEADS, PEER_KEYS, PEER_TOPK = 8, 128, 16
LANES = 128
BF16_ROWS = 16
VMEM_LIMIT = 56 * 1024 * 1024

CAND_PAIRS = [(r1, r2) for r1 in range(PEER_TOPK) for r2 in range(PEER_TOPK)
              if (r1 + 1) * (r2 + 1) <= PEER_TOPK]


def _nt(a, b, **kw):
    return lax.dot_general(a, b, (((1,), (1,)), ((), ())), preferred_element_type=F32, **kw)


def _tn(a, b, **kw):
    return lax.dot_general(a, b, (((0,), (0,)), ((), ())), preferred_element_type=F32, **kw)


def _mm(a, b, **kw):
    return jnp.dot(a, b, preferred_element_type=F32, **kw)


def _rms(x, g):
    return x * lax.rsqrt(jnp.mean(x * x, axis=-1, keepdims=True) + EPS) * g


def _silu(x):
    return x * jax.nn.sigmoid(x)


def _pack_pair(lo, hi):
    bits = lambda v: lax.bitcast_convert_type(v.astype(BF16).astype(F32), PACKED)
    return (bits(hi) & jnp.uint32(0xFFFF0000)) | lax.shift_right_logical(bits(lo), jnp.uint32(16))


def _unpack_pair(packed, half):
    bits = (packed & jnp.uint32(0xFFFF0000)) if half else lax.shift_left(packed, jnp.uint32(16))
    return lax.bitcast_convert_type(bits, F32)


def _params(sem, flags=None):
    return pltpu.CompilerParams(dimension_semantics=sem, vmem_limit_bytes=VMEM_LIMIT, flags=flags)


def _inproj_kernel(x_ref, g_ref, w_ref, o_ref):
    o_ref[...] = _mm(_rms(x_ref[...], g_ref[...]).astype(BF16), w_ref[...])


def _inproj(h, g, w, tm, tn):
    T, D = h.shape
    N = w.shape[1]
    return pl.pallas_call(
        _inproj_kernel,
        grid=(N // tn, T // tm),
        in_specs=[pl.BlockSpec((tm, D), lambda j, i: (i, 0)),
                  pl.BlockSpec((1, D), lambda j, i: (0, 0)),
                  pl.BlockSpec((D, tn), lambda j, i: (0, j))],
        out_specs=pl.BlockSpec((tm, tn), lambda j, i: (i, j)),
        out_shape=jax.ShapeDtypeStruct((T, N), F32),
        compiler_params=_params(("parallel", "parallel")),
        name="inproj",
    )(h, g, w)


def _split3(x):
    x1 = x.astype(BF16)
    r1 = x - x1.astype(F32)
    x2 = r1.astype(BF16)
    x3 = (r1 - x2.astype(F32)).astype(BF16)
    return x1, x2, x3


def _mm_sel(x, sel):
    x1, x2, x3 = _split3(x)
    return (_mm(x3, sel) + _mm(x2, sel)) + _mm(x1, sel)


def _sel_mm(sel, x):
    x1, x2, x3 = _split3(x)
    return (_mm(sel, x3) + _mm(sel, x2)) + _mm(sel, x1)


def _sel_nt(sel, x):
    x1, x2, x3 = _split3(x)
    return (_nt(sel, x3) + _nt(sel, x2)) + _nt(sel, x1)


def _ssd_kernel(z_ref, xs_ref, bc_ref, dt_ref, conv0_ref, h0_ref, cw_ref, cb_ref, dtb_ref, alog_ref,
                dexp_ref, ng_ref, e_ref, et_ref,
                y_ref, nconv_ref, ht_ref,
                xpad_ref, st_ref, yscr_ref, *, L, nseq):
    c = pl.program_id(1)
    inner = SSD_HEADS * SSD_HEAD_DIM
    hist = SSD_CONV - 1
    base = 8
    gn = SSD_GROUPS * SSD_STATE
    gw = inner // SSD_GROUPS
    heads_per_group = SSD_HEADS // SSD_GROUPS

    @pl.when(c == 0)
    def _():
        xpad_ref[:, base - hist:base, :] = conv0_ref[...]
        st_ref[...] = h0_ref[...]

    row = lax.broadcasted_iota(jnp.int32, (L, L), 0)
    col = lax.broadcasted_iota(jnp.int32, (L, L), 1)
    causal = row >= col
    tri = jnp.where(causal, 1.0, 0.0).astype(BF16)
    lane = lax.broadcasted_iota(jnp.int32, (L, 2 * SSD_HEAD_DIM), 1)
    a_neg = -jnp.exp(alog_ref[...])
    e = e_ref[...]

    for q in range(nseq):
        rows = slice(q * L, (q + 1) * L)
        xpad_ref[q, base:base + L, 0:inner] = xs_ref[rows, :]
        xpad_ref[q, base:base + L, inner:] = bc_ref[rows, :]

        conv = cb_ref[...] + cw_ref[0:1, :] * xpad_ref[q, base - hist:base - hist + L, :]
        for k in range(1, SSD_CONV):
            conv = conv + cw_ref[k:k + 1, :] * xpad_ref[q, base - hist + k:base - hist + k + L, :]
        last_rows = xpad_ref[q, base + L - hist:base + L, :]
        nconv_ref[q] = last_rows
        xpad_ref[q, base - hist:base, :] = last_rows
        xc = _silu(conv)
        xs = xc[:, :inner]
        bm = xc[:, inner:inner + gn].astype(BF16)
        cm = xc[:, inner + gn:].astype(BF16)

        x = dt_ref[rows, :] + dtb_ref[...]
        dt = jnp.maximum(x, 0.0) + jnp.log1p(jnp.exp(-jnp.abs(x)))
        acs = _sel_mm(tri, dt * a_neg)
        dt_x = _mm_sel(dt, e)
        acs_x = _mm_sel(acs, e)
        acs_t = _sel_nt(et_ref[...], acs)
        alast_x = acs_x[L - 1:L, :]
        xd = xs * dt_x
        xd_bf = xd.astype(BF16)
        xdec = (xd * jnp.exp(alast_x - acs_x)).astype(BF16)
        eacs = jnp.exp(acs_x)

        for g in range(SSD_GROUPS):
            bg = bm[:, g * SSD_STATE:(g + 1) * SSD_STATE]
            cg = cm[:, g * SSD_STATE:(g + 1) * SSD_STATE]
            cb = _nt(cg, bg)
            for pp in range(heads_per_group // 2):
                p = g * (heads_per_group // 2) + pp
                cols = slice(p * 2 * SSD_HEAD_DIM, (p + 1) * 2 * SSD_HEAD_DIM)
                xp = xd_bf[:, cols]
                ys = []
                for hh in range(2):
                    h = 2 * p + hh
                    acol = acs[:, h:h + 1]
                    arow = acs_t[h * SSD_HEAD_DIM:h * SSD_HEAD_DIM + 1, :]
                    lm = jnp.where(causal, jnp.exp(acol - arow), 0.0)
                    ys.append(_mm((cb * lm).astype(BF16), xp))
                ydiag = jnp.where(lane < SSD_HEAD_DIM, ys[0], ys[1])
                st = st_ref[q, p]
                yoff = _nt(cg, st.astype(BF16)) * eacs[:, cols]
                yscr_ref[rows, cols] = ydiag + yoff
                scale = jnp.exp(acs_t[p * 2 * SSD_HEAD_DIM:(p + 1) * 2 * SSD_HEAD_DIM, L - 1:L])
                st_ref[q, p] = st * scale + _tn(xdec[:, cols], bg)

        y = (yscr_ref[rows, :] + dexp_ref[...] * xs) * _silu(z_ref[rows, :])
        for g in range(SSD_GROUPS):
            seg = y[:, g * gw:(g + 1) * gw]
            seg = seg * lax.rsqrt(jnp.mean(seg * seg, axis=-1, keepdims=True) + EPS)
            y_ref[rows, g * gw:(g + 1) * gw] = seg * ng_ref[:, g * gw:(g + 1) * gw]

    ht_ref[...] = st_ref[...]


def _ssd(proj, conv0, h0, layer, wts, *, nb, nc, L, row_off, nseq):
    assert nb % nseq == 0 and (nseq == 1 or nc == 1) and row_off % (nseq * L) == 0
    inner = SSD_HEADS * SSD_HEAD_DIM
    cd = conv0.shape[-1]
    R = nseq * L
    r0 = row_off // R
    rows = lambda b, c: r0 + b * nc + c
    dt_blk = (proj.shape[1] - LANES) // LANES
    const = lambda shape: pl.BlockSpec(shape, lambda b, c: (0,) * len(shape))
    npair = SSD_HEADS // 2
    y, nconv, ht = pl.pallas_call(
        functools.partial(_ssd_kernel, L=L, nseq=nseq),
        grid=(nb // nseq, nc),
        in_specs=[pl.BlockSpec((R, inner), lambda b, c: (rows(b, c), 0)),
                  pl.BlockSpec((R, inner), lambda b, c: (rows(b, c), 1)),
                  pl.BlockSpec((R, inner), lambda b, c: (rows(b, c), 2)),
                  pl.BlockSpec((R, LANES), lambda b, c: (rows(b, c), dt_blk)),
                  pl.BlockSpec((None, nseq, SSD_CONV - 1, cd), lambda b, c: (layer, b, 0, 0)),
                  pl.BlockSpec((None, nseq, npair, LANES, SSD_STATE), lambda b, c: (layer, b, 0, 0, 0)),
                  const((SSD_CONV, cd)), const((1, cd)), const((1, LANES)), const((1, LANES)),
                  const((1, inner)), const((1, inner)), const((LANES, inner)), const((inner, LANES))],
        out_specs=[pl.BlockSpec((R, inner), lambda b, c: (b * nc + c, 0)),
                   pl.BlockSpec((nseq, SSD_CONV - 1, cd), lambda b, c: (b, 0, 0)),
                   pl.BlockSpec((nseq, npair, LANES, SSD_STATE), lambda b, c: (b, 0, 0, 0))],
        out_shape=[jax.ShapeDtypeStruct((nb * nc * L, inner), F32),
                   jax.ShapeDtypeStruct((nb, SSD_CONV - 1, cd), F32),
                   jax.ShapeDtypeStruct((nb, npair, LANES, SSD_STATE), F32)],
        scratch_shapes=[pltpu.VMEM((nseq, 8 + L, cd), F32),
                        pltpu.VMEM((nseq, npair, LANES, SSD_STATE), F32),
                        pltpu.VMEM((R, inner), F32)],
        compiler_params=_params(("parallel", "arbitrary")),
        name=f"ssd_L{L}",
    )(proj, proj, proj, proj, conv0, h0.reshape(-1, nb, npair, LANES, SSD_STATE), *wts)
    return y, nconv, ht.reshape(nb, SSD_HEADS, SSD_HEAD_DIM, SSD_STATE)


def _cf_kernel(a_ref, g_ref, buf0_ref, w_ref, b_ref, lng_ref, lnb_ref, y_ref, nbuf_ref,
               upad_ref, shift_ref, conv_ref, *, L):
    c = pl.program_id(1)
    hist = CF_CONV - 1
    base = 32
    dim = a_ref.shape[1]

    @pl.when(c == 0)
    def _():
        upad_ref[base - hist:base, :] = buf0_ref[0]

    upad_ref[base:base + L, :] = a_ref[...] * jax.nn.sigmoid(g_ref[...])
    first = base - hist
    span = shift_ref.shape[1]
    slab = 2 * LANES
    for s in range(dim // slab):
        cols = slice(s * slab, (s + 1) * slab)
        acc = jnp.broadcast_to(b_ref[:, cols], (L, slab))
        for phase in range(8):
            if phase:
                shift_ref[phase - 1, :, cols] = upad_ref[phase:phase + span, cols]
            for k in range(CF_CONV):
                if (first + k) % 8 == phase:
                    r0 = first + k - phase
                    src = upad_ref[r0:r0 + L, cols] if phase == 0 else shift_ref[phase - 1, r0:r0 + L, cols]
                    acc = acc + w_ref[k:k + 1, cols] * src
        conv_ref[:, cols] = acc
    nbuf = upad_ref[base + L - hist:base + L, :]
    nbuf_ref[0] = nbuf
    upad_ref[base - hist:base, :] = nbuf

    x = conv_ref[...]
    xc = x - jnp.mean(x, axis=-1, keepdims=True)
    y = xc * lax.rsqrt(jnp.mean(xc * xc, axis=-1, keepdims=True) + EPS) * lng_ref[...] + lnb_ref[...]
    y_ref[...] = _silu(y)


def _conformer(proj, buf0, layer, wts, *, nb, nc, L, row_off):
    dim = buf0.shape[-1]
    r0 = row_off // L
    rows = lambda b, c: r0 + b * nc + c
    const = lambda shape: pl.BlockSpec(shape, lambda b, c: (0,) * len(shape))
    return pl.pallas_call(
        functools.partial(_cf_kernel, L=L),
        grid=(nb, nc),
        in_specs=[pl.BlockSpec((L, dim), lambda b, c: (rows(b, c), 3)),
                  pl.BlockSpec((L, dim), lambda b, c: (rows(b, c), 4)),
                  pl.BlockSpec((None, 1, CF_CONV - 1, dim), lambda b, c: (layer, b, 0, 0)),
                  const((CF_CONV, dim)), const((1, dim)), const((1, dim)), const((1, dim))],
        out_specs=[pl.BlockSpec((L, dim), lambda b, c: (b * nc + c, 0)),
                   pl.BlockSpec((1, CF_CONV - 1, dim), lambda b, c: (b, 0, 0))],
        out_shape=[jax.ShapeDtypeStruct((nb * nc * L, dim), F32),
                   jax.ShapeDtypeStruct((nb, CF_CONV - 1, dim), F32)],
        scratch_shapes=[pltpu.VMEM((32 + L, dim), F32), pltpu.VMEM((7, 24 + L, dim), F32),
                        pltpu.VMEM((L, dim), F32)],
        compiler_params=_params(("parallel", "arbitrary")),
        name=f"conformer_L{L}",
    )(proj, proj, buf0, *wts)


def _merge_kernel(h_ref, yap_ref, yas_ref, ybp_ref, ybs_ref, ga_ref, gb_ref, wa_ref, wb_ref, wo_ref, o_ref,
                  *, split):
    prompt = pl.program_id(0) < split
    ya = jnp.where(prompt, yap_ref[...], yas_ref[...])
    yb = jnp.where(prompt, ybp_ref[...], ybs_ref[...])
    mix = (jax.nn.sigmoid(ga_ref[...]) * _mm(ya.astype(BF16), wa_ref[...])
           + jax.nn.sigmoid(gb_ref[...]) * _mm(yb.astype(BF16), wb_ref[...]))
    o_ref[...] = h_ref[...] + _mm(mix.astype(BF16), wo_ref[...])


def _merge(h, ya_p, ya_s, yb_p, yb_s, proj, wa, wb, wo, tm):
    T, D = h.shape
    split = ya_p.shape[0] // tm
    rowblk = lambda j: pl.BlockSpec((tm, D), lambda i: (i, j))
    pblk = pl.BlockSpec((tm, D), lambda i: (jnp.minimum(i, split - 1), 0))
    sblk = pl.BlockSpec((tm, D), lambda i: (jnp.maximum(i - split, 0), 0))
    wspec = pl.BlockSpec((D, D), lambda i: (0, 0))
    return pl.pallas_call(
        functools.partial(_merge_kernel, split=split),
        grid=(T // tm,),
        in_specs=[rowblk(0), pblk, sblk, pblk, sblk, rowblk(5), rowblk(6), wspec, wspec, wspec],
        out_specs=rowblk(0),
        out_shape=jax.ShapeDtypeStruct((T, D), F32),
        compiler_params=_params(("parallel",)),
        name="merge",
    )(h, ya_p, ya_s, yb_p, yb_s, proj, proj, wa, wb, wo)


def _extract_topk(s_ref, rank_ref, vals_ref):
    rank_ref[...] = jnp.full(rank_ref.shape, float(PEER_TOPK), F32)
    n = s_ref.shape[0]

    def step(k, carry):
        s = s_ref[...]
        m = jnp.max(s, axis=0)
        idx = lax.broadcasted_iota(jnp.int32, s.shape, 0)
        first = jnp.min(jnp.where(s == m[None], idx, n), axis=0)
        hit = idx == first[None]
        rank_ref[...] = jnp.where(hit, jnp.asarray(k, F32), rank_ref[...])
        s_ref[...] = jnp.where(hit, -jnp.inf, s)
        vals_ref[k] = m
        return carry

    lax.fori_loop(0, PEER_TOPK, step, 0)


def _sort16_desc(x):
    x = list(x)
    n = len(x)
    k = 2
    while k <= n:
        j = k // 2
        while j >= 1:
            for i in range(n):
                l = i ^ j
                if l > i:
                    hi, lo = jnp.maximum(x[i], x[l]), jnp.minimum(x[i], x[l])
                    x[i], x[l] = (hi, lo) if (i & k) == 0 else (lo, hi)
            j //= 2
        k *= 2
    return x


def _merge_top16(a, b):
    n = len(a)
    x = [jnp.maximum(a[i], b[n - 1 - i]) for i in range(n)]
    j = n // 2
    while j >= 1:
        for i in range(n):
            l = i ^ j
            if l > i:
                x[i], x[l] = jnp.maximum(x[i], x[l]), jnp.minimum(x[i], x[l])
        j //= 2
    return x


def _top16_sorted(rows):
    groups = [_sort16_desc(rows[g:g + PEER_TOPK]) for g in range(0, len(rows), PEER_TOPK)]
    while len(groups) > 1:
        groups = [_merge_top16(groups[i], groups[i + 1]) for i in range(0, len(groups), 2)]
    return groups[0]


def _select_kernel(h_ref, g_ref, wq_ref, kb1_ref, kb2_ref,
                   xnt_ref, n_ref, e1_ref, r2_ref, e2_ref,
                   s1_scr, s2_scr, w_scr, rk1_scr, rk2_scr, v1_scr, v2_scr,
                   cw_scr, crk_scr, cv_scr, pk_scr, *, tb):
    H, K = PEER_HEADS, PEER_KEYS
    half = H * K
    xn = _rms(h_ref[...], g_ref[...])
    xnt_ref[...] = xn.T.astype(BF16)
    q = _mm(xn.astype(BF16), wq_ref[...]).astype(BF16)
    s1 = _nt(kb1_ref[...], q[:, :half])
    s2 = _nt(kb2_ref[...], q[:, half:])
    nlt = tb // LANES
    for lt in range(nlt):
        s1_scr[lt] = s1[:, lt * LANES:(lt + 1) * LANES].reshape(K, H, LANES)
        s2_scr[lt] = s2[:, lt * LANES:(lt + 1) * LANES].reshape(K, H, LANES)

    topk = float(PEER_TOPK)

    def by_value(lt):
        a = _top16_sorted([s1_scr[lt, k] for k in range(K)])
        b = _top16_sorted([s2_scr[lt, k] for k in range(K)])
        cands = [a[r1] + b[r2] for r1, r2 in CAND_PAIRS]
        pad = [jnp.full((H, LANES), -jnp.inf, F32)] * (-len(cands) % PEER_TOPK)
        t = _top16_sorted(cands + pad)
        won = [jnp.where(c >= t[-1], 1.0, 0.0) for c in cands]
        ncnt = [sum(w for w, (r1, _) in zip(won, CAND_PAIRS) if r1 == r) for r in range(PEER_TOPK)]
        tie = jnp.where(sum(won) != topk, 1.0, 0.0)
        for v in (a, b, t):
            for k in range(PEER_TOPK - 1):
                tie = jnp.where(v[k] == v[k + 1], 1.0, tie)
        rz = 1.0 / sum(jnp.exp(v - t[0]) for v in t)

        def keys0(i, cnt):
            s = s1_scr[lt, i]
            n_i = jnp.zeros_like(s)
            for r in range(PEER_TOPK):
                n_i = jnp.where(s == a[r], ncnt[r], n_i)
            rows = pl.ds(pl.multiple_of(i * H, H), H)
            pk_scr[0, rows, :] = n_i
            pk_scr[1, rows, :] = jnp.exp(s - a[0]) * rz
            return cnt + jnp.where(s >= a[-1], 1.0, 0.0)

        def keys1(j, cnt):
            s = s2_scr[lt, j]
            rank = jnp.full_like(s, topk)
            for r in range(PEER_TOPK):
                rank = jnp.where(s == b[r], float(r), rank)
            rows = pl.ds(pl.multiple_of(j * H, H), H)
            pk_scr[2, rows, :] = rank
            pk_scr[3, rows, :] = jnp.exp(s - b[0])
            return cnt + jnp.where(s >= b[-1], 1.0, 0.0)

        zero = jnp.zeros((H, LANES), F32)
        for body in (keys0, keys1):
            tie = jnp.where(lax.fori_loop(0, K, body, zero, unroll=8) != topk, 1.0, tie)
        return tie

    def by_extraction(lt):
        w_scr[...] = s1_scr[lt]
        _extract_topk(w_scr, rk1_scr, v1_scr)
        w_scr[...] = s2_scr[lt]
        _extract_topk(w_scr, rk2_scr, v2_scr)
        for p, (r1, r2) in enumerate(CAND_PAIRS):
            cw_scr[p] = v1_scr[r1] + v2_scr[r2]
        _extract_topk(cw_scr, crk_scr, cv_scr)
        top = cv_scr[0]
        zsum = jnp.zeros_like(top)
        for k in range(PEER_TOPK):
            zsum = zsum + jnp.exp(cv_scr[k] - top)
        rz = 1.0 / zsum
        rk1 = rk1_scr[...]
        n_i = jnp.zeros_like(rk1)
        for r1 in range(PEER_TOPK):
            cnt = jnp.zeros_like(top)
            for p, (a, _) in enumerate(CAND_PAIRS):
                if a == r1:
                    cnt = cnt + jnp.where(crk_scr[p] < topk, 1.0, 0.0)
            n_i = jnp.where(rk1 == float(r1), cnt[None], n_i)
        pk_scr[0] = n_i.reshape(half, LANES)
        pk_scr[1] = (jnp.exp(s1_scr[lt] - v1_scr[0][None]) * rz[None]).reshape(half, LANES)
        pk_scr[2] = rk2_scr[...].reshape(half, LANES)
        pk_scr[3] = jnp.exp(s2_scr[lt] - v2_scr[0][None]).reshape(half, LANES)

    def lane_tile(lt, carry):
        tie = by_value(lt)

        @pl.when(jnp.max(tie) > 0.0)
        def _():
            by_extraction(lt)

        lanes = pl.ds(pl.multiple_of(lt * LANES, LANES), LANES)
        for h in range(H):
            n_ref[h, :, lanes] = pk_scr[0, pl.ds(h, K, stride=H), :]
            e1_ref[h, :, lanes] = pk_scr[1, pl.ds(h, K, stride=H), :]
            for a, o_ref in ((2, r2_ref), (3, e2_ref)):
                lo = pk_scr[a, pl.ds(h, K // 2, stride=H), :]
                hi = pk_scr[a, pl.ds(h + (K // 2) * H, K // 2, stride=H), :]
                packed = _pack_pair(lo, hi)
                for jt in range(K // BF16_ROWS):
                    o_ref[lt, jt, h] = packed[jt * 8:(jt + 1) * 8]
        return carry

    lax.fori_loop(0, nlt, lane_tile, 0)


def _peer_select(h1, g, wq, kb1, kb2, tb):
    T, D = h1.shape
    H, K = PEER_HEADS, PEER_KEYS
    nlt = tb // LANES
    ncand = len(CAND_PAIRS)
    const = lambda shape: pl.BlockSpec(shape, lambda i: (0,) * len(shape))
    hkt = jax.ShapeDtypeStruct((H, K, T), F32)
    hkt_spec = pl.BlockSpec((H, K, tb), lambda i: (0, 0, i))
    pair = jax.ShapeDtypeStruct((T // LANES, K // BF16_ROWS, H, 8, LANES), PACKED)
    pair_spec = pl.BlockSpec((nlt, K // BF16_ROWS, H, 8, LANES), lambda i: (i, 0, 0, 0, 0))
    tile = lambda n: pltpu.VMEM((n, H, LANES), F32)
    return pl.pallas_call(
        functools.partial(_select_kernel, tb=tb),
        grid=(T // tb,),
        in_specs=[pl.BlockSpec((tb, D), lambda i: (i, 0)), const((1, D)), const(wq.shape),
                  const(kb1.shape), const(kb2.shape)],
        out_specs=[pl.BlockSpec((D, tb), lambda i: (0, i)), hkt_spec, hkt_spec, pair_spec, pair_spec],
        out_shape=[jax.ShapeDtypeStruct((D, T), BF16), hkt, hkt, pair, pair],
        scratch_shapes=[pltpu.VMEM((nlt, K, H, LANES), F32), pltpu.VMEM((nlt, K, H, LANES), F32),
                        tile(K), tile(K), tile(K), tile(PEER_TOPK), tile(PEER_TOPK),
                        tile(ncand), tile(ncand), tile(PEER_TOPK),
                        pltpu.VMEM((4, H * K, LANES), F32)],
        compiler_params=_params(("parallel",)),
        name="peer_select",
    )(h1, g, wq, kb1, kb2)


def _gelu_tanh(x):
    c = math.sqrt(2.0 / math.pi)
    hx = 0.5 * x
    return hx * jnp.tanh(x * (c + (c * 0.044715) * (x * x))) + hx


def _dense_kernel(h_ref, xnt_ref, n_ref, e1_ref, r2_ref, e2_ref, u_ref, vt_ref, o_ref,
                  pre0_scr, pre1_scr, act0_scr, act1_scr, acc_scr, *, tb, ec):
    k = pl.program_id(1)
    last = pl.num_programs(1) - 1
    H, K = PEER_HEADS, PEER_KEYS
    ni = ec // K
    zero = jnp.zeros((), BF16)
    lw = 2 * LANES

    def stage_b(cb, pre_ref, act_ref, lp):
        for lt in range(lw // LANES):
            ltile = lp * (lw // LANES) + lt
            lanes = pl.ds(pl.multiple_of(ltile * LANES, LANES), LANES)
            for i8 in range(ni // 8):
                irows = pl.ds(pl.multiple_of(cb * ni + i8 * 8, 8), 8)
                n8 = [n_ref[h, irows, lanes] for h in range(H)]
                e8 = [e1_ref[h, irows, lanes] for h in range(H)]
                for i1 in range(8):
                    il = i8 * 8 + i1
                    nb = [jnp.broadcast_to(n8[h][i1:i1 + 1, :], (BF16_ROWS, LANES)).astype(BF16) for h in range(H)]
                    eb = [jnp.broadcast_to(e8[h][i1:i1 + 1, :], (BF16_ROWS, LANES)).astype(BF16) for h in range(H)]
                    for jt in range(K // BF16_ROWS):
                        gate = None
                        for h in range(H):
                            r = pltpu.bitcast(r2_ref[ltile, jt, h], BF16)
                            e = pltpu.bitcast(e2_ref[ltile, jt, h], BF16)
                            t = jnp.where(r < nb[h], e, zero) * eb[h]
                            gate = t if gate is None else gate + t
                        gate = pltpu.bitcast(gate, PACKED)
                        for half in range(2):
                            g = _unpack_pair(gate, half)
                            r0 = il * K + half * (K // 2) + jt * 8
                            act_ref[r0:r0 + 8, lanes] = (_gelu_tanh(pre_ref[r0:r0 + 8, lanes]) * g).astype(BF16)

    def sub_step(sub, do_a, do_b, do_c):
        s = 2 * k + sub
        first = s == 2
        pre_w, pre_r = (pre0_scr, pre1_scr) if sub == 0 else (pre1_scr, pre0_scr)
        act_r, act_w = (act0_scr, act1_scr) if sub == 0 else (act1_scr, act0_scr)

        def body(lp, carry):
            lanes = pl.ds(pl.multiple_of(lp * lw, lw), lw)
            if do_a:
                u = pltpu.bitcast(u_ref[sub * (ec // 2):(sub + 1) * (ec // 2), :], BF16)
                pre_w[:, lanes] = _mm(u, xnt_ref[:, lanes])
            if do_b:
                stage_b(s - 1, pre_r, act_w, lp)
            if do_c:
                vt = pltpu.bitcast(vt_ref[:, sub * ec:(sub + 1) * ec], BF16)
                contrib = _mm(vt, act_r[:, lanes])
                acc_scr[:, lanes] = jnp.where(first, contrib, acc_scr[:, lanes] + contrib)
            return carry

        lax.fori_loop(0, tb // lw, body, 0)

    @pl.when(k == 0)
    def _():
        sub_step(0, True, False, False)
        sub_step(1, True, True, False)

    @pl.when((k > 0) & (k < last))
    def _():
        sub_step(0, True, True, True)
        sub_step(1, True, True, True)

    @pl.when(k == last)
    def _():
        sub_step(0, False, True, True)
        sub_step(1, False, False, True)

    @pl.when(k == last)
    def _():
        o_ref[...] = h_ref[...] + acc_scr[...].T


def _pack_kernel(w_ref, o_ref, *, transpose):
    w = w_ref[...]
    o_ref[...] = pltpu.bitcast((w.T if transpose else w).astype(BF16), PACKED)


def _pack_table(w, layer, *, transpose, rows=1024):
    _, E, D = w.shape
    if transpose:
        out_shape, out_spec = (D // 2, E), pl.BlockSpec((D // 2, rows), lambda i: (0, i))
    else:
        out_shape, out_spec = (E // 2, D), pl.BlockSpec((rows // 2, D), lambda i: (i, 0))
    return pl.pallas_call(
        functools.partial(_pack_kernel, transpose=transpose),
        grid=(E // rows,),
        in_specs=[pl.BlockSpec((None, rows, D), lambda i: (layer, i, 0))],
        out_specs=out_spec,
        out_shape=jax.ShapeDtypeStruct(out_shape, PACKED),
        compiler_params=_params(("parallel",)),
        name="pack_table_t" if transpose else "pack_table",
    )(w)


def _peer_dense(h1, xnt, n, e1, r2, e2, u, vt, tb, ec):
    T, D = h1.shape
    H, K = PEER_HEADS, PEER_KEYS
    nc = 2 * u.shape[0] // ec
    assert nc % 2 == 0 and nc >= 4
    nsteps = nc // 2 + 1
    hkt_spec = pl.BlockSpec((H, K, tb), lambda i, k: (0, 0, i))
    pair_spec = pl.BlockSpec((tb // LANES, K // BF16_ROWS, H, 8, LANES), lambda i, k: (i, 0, 0, 0, 0))
    return pl.pallas_call(
        functools.partial(_dense_kernel, tb=tb, ec=ec),
        grid=(T // tb, nsteps),
        in_specs=[pl.BlockSpec((tb, D), lambda i, k: (i, 0)),
                  pl.BlockSpec((D, tb), lambda i, k: (0, i)),
                  hkt_spec, hkt_spec, pair_spec, pair_spec,
                  pl.BlockSpec((ec, D), lambda i, k: (jnp.minimum(k, nsteps - 2), 0)),
                  pl.BlockSpec((D // 2, 2 * ec), lambda i, k: (0, jnp.maximum(k - 1, 0)))],
        out_specs=pl.BlockSpec((tb, D), lambda i, k: (i, 0)),
        out_shape=jax.ShapeDtypeStruct((T, D), F32),
        scratch_shapes=[pltpu.VMEM((ec, tb), F32), pltpu.VMEM((ec, tb), F32),
                        pltpu.VMEM((ec, tb), BF16), pltpu.VMEM((ec, tb), BF16), pltpu.VMEM((D, tb), F32)],
        compiler_params=_params(("arbitrary", "arbitrary")),
        name="peer_dense",
    )(h1, xnt, n, e1, r2, e2, u, vt)


def _ple_kernel(h_ref, p_ref, g_ref, wg_ref, wp_ref, gf_ref, *o_refs, split):
    h = h_ref[...]
    gate = jax.nn.sigmoid(_mm(_rms(h, g_ref[...]).astype(BF16), wg_ref[...]))
    h = h + gate * _mm(p_ref[...].astype(BF16), wp_ref[...])
    if split is None:
        o_refs[0][...] = h
        return
    y = _rms(h, gf_ref[...])
    o_refs[1][...] = y

    @pl.when(pl.program_id(0) < split)
    def _():
        o_refs[0][...] = y


def _ple(h, p, g, wg, wp, gf, tm, split_rows=None):
    T, D = h.shape
    P = p.shape[1]
    const = lambda shape: pl.BlockSpec(shape, lambda i: (0,) * len(shape))
    tile = lambda fn: pl.BlockSpec((tm, D), fn)
    if split_rows is None:
        split, out_specs, out_shape = None, tile(lambda i: (i, 0)), jax.ShapeDtypeStruct((T, D), F32)
    else:
        split = split_rows // tm
        out_specs = [tile(lambda i: (jnp.minimum(i, split - 1), 0)), tile(lambda i: (jnp.maximum(i - split, 0), 0))]
        out_shape = [jax.ShapeDtypeStruct((split_rows, D), F32), jax.ShapeDtypeStruct((T - split_rows, D), F32)]
    return pl.pallas_call(
        functools.partial(_ple_kernel, split=split),
        grid=(T // tm,),
        in_specs=[tile(lambda i: (i, 0)), pl.BlockSpec((tm, P), lambda i: (i, 0)),
                  const((1, D)), const((D, D)), const((P, D)), const((1, D))],
        out_specs=out_specs,
        out_shape=out_shape,
        compiler_params=_params(("arbitrary",)),
        name="ple",
    )(h, p, g, wg, wp, gf)


def _tiles(tp, ts):
    tm = next(t for t in (512, 256, 128, 64, 32, 16, 8) if tp % t == 0 and ts % t == 0)
    tb = next(t for t in (512, 256) if (tp + ts) % t == 0)
    return tm, tb


def _pad_lanes(v, fill=0.0):
    return jnp.pad(v.astype(F32), (0, LANES - v.shape[0]), constant_values=fill)[None, :]


def kernel(x_prompt, x_sample, state_ssd, state_ssd_conv, state_cf_conv, p_prompt, p_sample, g_mix, w_in, ssd_conv_w, ssd_conv_b, ssd_dt_bias, ssd_a_log, ssd_d, ssd_norm_g, w_ssd_out, cf_dw_w, cf_dw_b, cf_ln_g, cf_ln_b, w_cf_out, w_o, g_ffn, peer_wq, peer_keys, peer_u, peer_v, g_ple, w_ple_gate, w_ple_proj, g_final):
    depth = w_in.shape[0]
    bp, lp, D = x_prompt.shape
    bs, ls, _ = x_sample.shape
    tp, ts = bp * lp, bs * ls
    T = tp + ts
    inner = SSD_HEADS * SSD_HEAD_DIM
    cd = state_ssd_conv.shape[-1]
    H, K = PEER_HEADS, PEER_KEYS
    tm, tb = _tiles(tp, ts)
    lc_p, lc_s = min(SSD_CHUNK, lp), min(SSD_CHUNK, ls)
    assert lp % lc_p == 0 and ls % lc_s == 0 and tp % lc_s == 0 and D == inner == cf_dw_w.shape[-1]

    h = jnp.concatenate([x_prompt.reshape(tp, D), x_sample.reshape(ts, D)], axis=0)
    expand = (jnp.arange(inner)[None, :] // SSD_HEAD_DIM == jnp.arange(LANES)[:, None]).astype(BF16)
    seqs_per_step = next(n for n in (8, 4, 2, 1) if bs % n == 0 and tp % (n * lc_s) == 0) if ls == lc_s else 1
    zeros_conv = jnp.zeros((1, bp, SSD_CONV - 1, cd), F32)
    zeros_h = jnp.zeros((1, bp, SSD_HEADS, SSD_HEAD_DIM, SSD_STATE), F32)
    zeros_cf = jnp.zeros((1, bp, CF_CONV - 1, D), F32)
    row = lambda v: v.astype(F32)[None, :]
    eye_h = jnp.eye(H, dtype=F32)
    outs = {k: [] for k in ("ssd_p", "sconv_p", "cf_p", "ssd_s", "sconv_s", "cf_s")}

    for i in range(depth):
        w = w_in[i]
        o_xbc, o_dt = inner, inner + cd
        o_glu = o_dt + SSD_HEADS
        w_r = jnp.concatenate([w[:, :o_xbc], w[:, o_xbc:o_dt], w[:, o_glu:],
                               jnp.pad(w[:, o_dt:o_glu], ((0, 0), (0, LANES - SSD_HEADS)))], axis=1).astype(BF16)
        ncols = w_r.shape[1]
        tn = next(t for t in (ncols // 3, ncols) if t % LANES == 0 and ncols % t == 0)
        proj = _inproj(h, row(g_mix[i]), w_r, tm, tn)

        ssd_w = (ssd_conv_w[i], row(ssd_conv_b[i]), _pad_lanes(ssd_dt_bias[i]), _pad_lanes(ssd_a_log[i]),
                 row(jnp.repeat(ssd_d[i], SSD_HEAD_DIM)), row(ssd_norm_g[i]), expand, expand.T)
        ya_p, sconv_p, ssd_p = _ssd(proj, zeros_conv, zeros_h, 0, ssd_w, nb=bp, nc=lp // lc_p, L=lc_p, row_off=0,
                                    nseq=1)
        ya_s, sconv_s, ssd_s = _ssd(proj, state_ssd_conv, state_ssd, i, ssd_w,
                                    nb=bs, nc=ls // lc_s, L=lc_s, row_off=tp, nseq=seqs_per_step)
        cf_w = (cf_dw_w[i], row(cf_dw_b[i]), row(cf_ln_g[i]), row(cf_ln_b[i]))
        yb_p, cf_p = _conformer(proj, zeros_cf, 0, cf_w, nb=bp, nc=lp // lc_p, L=lc_p, row_off=0)
        yb_s, cf_s = _conformer(proj, state_cf_conv, i, cf_w, nb=bs, nc=ls // lc_s, L=lc_s, row_off=tp)
        for k, v in (("ssd_p", ssd_p), ("sconv_p", sconv_p), ("cf_p", cf_p),
                     ("ssd_s", ssd_s), ("sconv_s", sconv_s), ("cf_s", cf_s)):
            outs[k].append(v)
        h = _merge(h, ya_p, ya_s, yb_p, yb_s, proj,
                   w_ssd_out[i].astype(BF16), w_cf_out[i].astype(BF16), w_o[i].astype(BF16), tm)

        qd = peer_keys.shape[-1]
        wq = peer_wq[i].reshape(D, H, 2, qd).transpose(0, 2, 1, 3).reshape(D, 2 * H * qd).astype(BF16)
        kb = [jnp.einsum("hkd,hg->khgd", peer_keys[i, :, s], eye_h).reshape(K * H, H * qd).astype(BF16)
              for s in range(2)]
        xnt, n_sel, e1, r2, e2 = _peer_select(h, row(g_ffn[i]), wq, kb[0], kb[1], tb)
        h = _peer_dense(h, xnt, n_sel, e1, r2, e2, _pack_table(peer_u, i, transpose=False), _pack_table(peer_v, i, transpose=True),
                        tb, ec=8 * K)

        p_all = jnp.concatenate([p_prompt[i].reshape(tp, -1), p_sample[i].reshape(ts, -1)], axis=0)
        h = _ple(h, p_all, row(g_ple[i]), w_ple_gate[i].astype(BF16), w_ple_proj[i].astype(BF16),
                 row(g_final), tm, split_rows=tp if i == depth - 1 else None)

    y_prompt, y_sample = h
    st = {k: jnp.stack(v) for k, v in outs.items()}
    return (y_prompt.reshape(bp, lp, D), y_sample.reshape(bs, ls, D),
            st["ssd_p"], st["sconv_p"], st["cf_p"], st["ssd_s"], st["sconv_s"], st["cf_s"])
```

```python
import functools
import math

import jax
import jax.numpy as jnp
from jax import lax
from jax.experimental import pallas as pl
from jax.experimental.pallas import tpu as pltpu

F32, BF16 = jnp.float32, jnp.bfloat16
PACKED = jnp.uint32
EPS = 1e-6

SSD_HEADS, SSD_HEAD_DIM, SSD_GROUPS, SSD_STATE, SSD_CONV, SSD_CHUNK = 16, 64, 4, 128, 4, 128
CF_CONV = 31
PEER_HEADS, PEER_KEYS, PEER_TOPK = 8, 128, 16
LANES = 128
BF16_ROWS = 16
VMEM_LIMIT = 56 * 1024 * 1024

CAND_PAIRS = [(r1, r2) for r1 in range(PEER_TOPK) for r2 in range(PEER_TOPK)
              if (r1 + 1) * (r2 + 1) <= PEER_TOPK]


def _nt(a, b, **kw):
    return lax.dot_general(a, b, (((1,), (1,)), ((), ())), preferred_element_type=F32, **kw)


def _tn(a, b, **kw):
    return lax.dot_general(a, b, (((0,), (0,)), ((), ())), preferred_element_type=F32, **kw)


def _mm(a, b, **kw):
    return jnp.dot(a, b, preferred_element_type=F32, **kw)


def _rms(x, g):
    return x * lax.rsqrt(jnp.mean(x * x, axis=-1, keepdims=True) + EPS) * g


def _silu(x):
    return x * jax.nn.sigmoid(x)


def _pack_pair(lo, hi):
    bits = lambda v: lax.bitcast_convert_type(v.astype(BF16).astype(F32), PACKED)
    return (bits(hi) & jnp.uint32(0xFFFF0000)) | lax.shift_right_logical(bits(lo), jnp.uint32(16))


def _unpack_pair(packed, half):
    bits = (packed & jnp.uint32(0xFFFF0000)) if half else lax.shift_left(packed, jnp.uint32(16))
    return lax.bitcast_convert_type(bits, F32)


def _params(sem, flags=None):
    return pltpu.CompilerParams(dimension_semantics=sem, vmem_limit_bytes=VMEM_LIMIT, flags=flags)


def _inproj_kernel(x_ref, g_ref, w_ref, o_ref):
    o_ref[...] = _mm(_rms(x_ref[...], g_ref[...]).astype(BF16), w_ref[...])


def _inproj(h, g, w, tm, tn):
    T, D = h.shape
    N = w.shape[1]
    return pl.pallas_call(
        _inproj_kernel,
        grid=(N // tn, T // tm),
        in_specs=[pl.BlockSpec((tm, D), lambda j, i: (i, 0)),
                  pl.BlockSpec((1, D), lambda j, i: (0, 0)),
                  pl.BlockSpec((D, tn), lambda j, i: (0, j))],
        out_specs=pl.BlockSpec((tm, tn), lambda j, i: (i, j)),
        out_shape=jax.ShapeDtypeStruct((T, N), F32),
        compiler_params=_params(("parallel", "parallel")),
        name="inproj",
    )(h, g, w)


def _split3(x):
    x1 = x.astype(BF16)
    r1 = x - x1.astype(F32)
    x2 = r1.astype(BF16)
    x3 = (r1 - x2.astype(F32)).astype(BF16)
    return x1, x2, x3


def _mm_sel(x, sel):
    x1, x2, x3 = _split3(x)
    return (_mm(x3, sel) + _mm(x2, sel)) + _mm(x1, sel)


def _sel_mm(sel, x):
    x1, x2, x3 = _split3(x)
    return (_mm(sel, x3) + _mm(sel, x2)) + _mm(sel, x1)


def _sel_nt(sel, x):
    x1, x2, x3 = _split3(x)
    return (_nt(sel, x3) + _nt(sel, x2)) + _nt(sel, x1)


def _ssd_kernel(z_ref, xs_ref, bc_ref, dt_ref, conv0_ref, h0_ref, cw_ref, cb_ref, dtb_ref, alog_ref,
                dexp_ref, ng_ref, e_ref, et_ref,
                y_ref, nconv_ref, ht_ref,
                xpad_ref, st_ref, yscr_ref, *, L, nseq):
    c = pl.program_id(1)
    inner = SSD_HEADS * SSD_HEAD_DIM
    hist = SSD_CONV - 1
    base = 8
    gn = SSD_GROUPS * SSD_STATE
    gw = inner // SSD_GROUPS
    heads_per_group = SSD_HEADS // SSD_GROUPS

    @pl.when(c == 0)
    def _():
        xpad_ref[:, base - hist:base, :] = conv0_ref[...]
        st_ref[...] = h0_ref[...]

    row = lax.broadcasted_iota(jnp.int32, (L, L), 0)
    col = lax.broadcasted_iota(jnp.int32, (L, L), 1)
    causal = row >= col
    tri = jnp.where(causal, 1.0, 0.0).astype(BF16)
    lane = lax.broadcasted_iota(jnp.int32, (L, 2 * SSD_HEAD_DIM), 1)
    a_neg = -jnp.exp(alog_ref[...])
    e = e_ref[...]

    for q in range(nseq):
        rows = slice(q * L, (q + 1) * L)
        xpad_ref[q, base:base + L, 0:inner] = xs_ref[rows, :]
        xpad_ref[q, base:base + L, inner:] = bc_ref[rows, :]

        conv = cb_ref[...] + cw_ref[0:1, :] * xpad_ref[q, base - hist:base - hist + L, :]
        for k in range(1, SSD_CONV):
            conv = conv + cw_ref[k:k + 1, :] * xpad_ref[q, base - hist + k:base - hist + k + L, :]
        last_rows = xpad_ref[q, base + L - hist:base + L, :]
        nconv_ref[q] = last_rows
        xpad_ref[q, base - hist:base, :] = last_rows
        xc = _silu(conv)
        xs = xc[:, :inner]
        bm = xc[:, inner:inner + gn].astype(BF16)
        cm = xc[:, inner + gn:].astype(BF16)

        x = dt_ref[rows, :] + dtb_ref[...]
        dt = jnp.maximum(x, 0.0) + jnp.log1p(jnp.exp(-jnp.abs(x)))
        acs = _sel_mm(tri, dt * a_neg)
        dt_x = _mm_sel(dt, e)
        acs_x = _mm_sel(acs, e)
        acs_t = _sel_nt(et_ref[...], acs)
        alast_x = acs_x[L - 1:L, :]
        xd = xs * dt_x
        xd_bf = xd.astype(BF16)
        xdec = (xd * jnp.exp(alast_x - acs_x)).astype(BF16)
        eacs = jnp.exp(acs_x)

        for g in range(SSD_GROUPS):
            bg = bm[:, g * SSD_STATE:(g + 1) * SSD_STATE]
            cg = cm[:, g * SSD_STATE:(g + 1) * SSD_STATE]
            cb = _nt(cg, bg)
            for pp in range(heads_per_group // 2):
                p = g * (heads_per_group // 2) + pp
                cols = slice(p * 2 * SSD_HEAD_DIM, (p + 1) * 2 * SSD_HEAD_DIM)
                xp = xd_bf[:, cols]
                ys = []
                for hh in range(2):
                    h = 2 * p + hh
                    acol = acs[:, h:h + 1]
                    arow = acs_t[h * SSD_HEAD_DIM:h * SSD_HEAD_DIM + 1, :]
                    lm = jnp.where(causal, jnp.exp(acol - arow), 0.0)
                    ys.append(_mm((cb * lm).astype(BF16), xp))
                ydiag = jnp.where(lane < SSD_HEAD_DIM, ys[0], ys[1])
                st = st_ref[q, p]
                yoff = _nt(cg, st.astype(BF16)) * eacs[:, cols]
                yscr_ref[rows, cols] = ydiag + yoff
                scale = jnp.exp(acs_t[p * 2 * SSD_HEAD_DIM:(p + 1) * 2 * SSD_HEAD_DIM, L - 1:L])
                st_ref[q, p] = st * scale + _tn(xdec[:, cols], bg)

        y = (yscr_ref[rows, :] + dexp_ref[...] * xs) * _silu(z_ref[rows, :])
        for g in range(SSD_GROUPS):
            seg = y[:, g * gw:(g + 1) * gw]
            seg = seg * lax.rsqrt(jnp.mean(seg * seg, axis=-1, keepdims=True) + EPS)
            y_ref[rows, g * gw:(g + 1) * gw] = seg * ng_ref[:, g * gw:(g + 1) * gw]

    ht_ref[...] = st_ref[...]


def _ssd(proj, conv0, h0, layer, wts, *, nb, nc, L, row_off, nseq):
    assert nb % nseq == 0 and (nseq == 1 or nc == 1) and row_off % (nseq * L) == 0
    inner = SSD_HEADS * SSD_HEAD_DIM
    cd = conv0.shape[-1]
    R = nseq * L
    r0 = row_off // R
    rows = lambda b, c: r0 + b * nc + c
    dt_blk = (proj.shape[1] - LANES) // LANES
    const = lambda shape: pl.BlockSpec(shape, lambda b, c: (0,) * len(shape))
    npair = SSD_HEADS // 2
    y, nconv, ht = pl.pallas_call(
        functools.partial(_ssd_kernel, L=L, nseq=nseq),
        grid=(nb // nseq, nc),
        in_specs=[pl.BlockSpec((R, inner), lambda b, c: (rows(b, c), 0)),
                  pl.BlockSpec((R, inner), lambda b, c: (rows(b, c), 1)),
                  pl.BlockSpec((R, inner), lambda b, c: (rows(b, c), 2)),
                  pl.BlockSpec((R, LANES), lambda b, c: (rows(b, c), dt_blk)),
                  pl.BlockSpec((None, nseq, SSD_CONV - 1, cd), lambda b, c: (layer, b, 0, 0)),
                  pl.BlockSpec((None, nseq, npair, LANES, SSD_STATE), lambda b, c: (layer, b, 0, 0, 0)),
                  const((SSD_CONV, cd)), const((1, cd)), const((1, LANES)), const((1, LANES)),
                  const((1, inner)), const((1, inner)), const((LANES, inner)), const((inner, LANES))],
        out_specs=[pl.BlockSpec((R, inner), lambda b, c: (b * nc + c, 0)),
                   pl.BlockSpec((nseq, SSD_CONV - 1, cd), lambda b, c: (b, 0, 0)),
                   pl.BlockSpec((nseq, npair, LANES, SSD_STATE), lambda b, c: (b, 0, 0, 0))],
        out_shape=[jax.ShapeDtypeStruct((nb * nc * L, inner), F32),
                   jax.ShapeDtypeStruct((nb, SSD_CONV - 1, cd), F32),
                   jax.ShapeDtypeStruct((nb, npair, LANES, SSD_STATE), F32)],
        scratch_shapes=[pltpu.VMEM((nseq, 8 + L, cd), F32),
                        pltpu.VMEM((nseq, npair, LANES, SSD_STATE), F32),
                        pltpu.VMEM((R, inner), F32)],
        compiler_params=_params(("parallel", "arbitrary")),
        name=f"ssd_L{L}",
    )(proj, proj, proj, proj, conv0, h0.reshape(-1, nb, npair, LANES, SSD_STATE), *wts)
    return y, nconv, ht.reshape(nb, SSD_HEADS, SSD_HEAD_DIM, SSD_STATE)


def _cf_kernel(a_ref, g_ref, buf0_ref, w_ref, b_ref, lng_ref, lnb_ref, y_ref, nbuf_ref,
               upad_ref, shift_ref, conv_ref, *, L, nseq):
    c = pl.program_id(1)
    hist = CF_CONV - 1
    base = 32
    dim = a_ref.shape[1]

    @pl.when(c == 0)
    def _():
        upad_ref[:, base - hist:base, :] = buf0_ref[...]

    first = base - hist
    span = shift_ref.shape[2]
    slab = 2 * LANES
    for q in range(nseq):
        rows = slice(q * L, (q + 1) * L)
        upad_ref[q, base:base + L, :] = a_ref[rows, :] * jax.nn.sigmoid(g_ref[rows, :])
        for s in range(dim // slab):
            cols = slice(s * slab, (s + 1) * slab)
            acc = jnp.broadcast_to(b_ref[:, cols], (L, slab))
            for phase in range(8):
                if phase:
                    shift_ref[q, phase - 1, :, cols] = upad_ref[q, phase:phase + span, cols]
                for k in range(CF_CONV):
                    if (first + k) % 8 == phase:
                        r0 = first + k - phase
                        src = (upad_ref[q, r0:r0 + L, cols] if phase == 0
                               else shift_ref[q, phase - 1, r0:r0 + L, cols])
                        acc = acc + w_ref[k:k + 1, cols] * src
            conv_ref[rows, cols] = acc
        nbuf = upad_ref[q, base + L - hist:base + L, :]
        nbuf_ref[q] = nbuf
        upad_ref[q, base - hist:base, :] = nbuf

    x = conv_ref[...]
    xc = x - jnp.mean(x, axis=-1, keepdims=True)
    y = xc * lax.rsqrt(jnp.mean(xc * xc, axis=-1, keepdims=True) + EPS) * lng_ref[...] + lnb_ref[...]
    y_ref[...] = _silu(y)


def _conformer(proj, buf0, layer, wts, *, nb, nc, L, row_off, nseq):
    assert nb % nseq == 0 and (nseq == 1 or nc == 1) and row_off % (nseq * L) == 0
    dim = buf0.shape[-1]
    R = nseq * L
    r0 = row_off // R
    rows = lambda b, c: r0 + b * nc + c
    const = lambda shape: pl.BlockSpec(shape, lambda b, c: (0,) * len(shape))
    return pl.pallas_call(
        functools.partial(_cf_kernel, L=L, nseq=nseq),
        grid=(nb // nseq, nc),
        in_specs=[pl.BlockSpec((R, dim), lambda b, c: (rows(b, c), 3)),
                  pl.BlockSpec((R, dim), lambda b, c: (rows(b, c), 4)),
                  pl.BlockSpec((None, nseq, CF_CONV - 1, dim), lambda b, c: (layer, b, 0, 0)),
                  const((CF_CONV, dim)), const((1, dim)), const((1, dim)), const((1, dim))],
        out_specs=[pl.BlockSpec((R, dim), lambda b, c: (b * nc + c, 0)),
                   pl.BlockSpec((nseq, CF_CONV - 1, dim), lambda b, c: (b, 0, 0))],
        out_shape=[jax.ShapeDtypeStruct((nb * nc * L, dim), F32),
                   jax.ShapeDtypeStruct((nb, CF_CONV - 1, dim), F32)],
        scratch_shapes=[pltpu.VMEM((nseq, 32 + L, dim), F32), pltpu.VMEM((nseq, 7, 24 + L, dim), F32),
                        pltpu.VMEM((R, dim), F32)],
        compiler_params=_params(("parallel", "arbitrary")),
        name=f"conformer_L{L}",
    )(proj, proj, buf0, *wts)


def _merge_kernel(h_ref, yap_ref, yas_ref, ybp_ref, ybs_ref, ga_ref, gb_ref, wa_ref, wb_ref, wo_ref, o_ref,
                  *, split):
    prompt = pl.program_id(0) < split
    ya = jnp.where(prompt, yap_ref[...], yas_ref[...])
    yb = jnp.where(prompt, ybp_ref[...], ybs_ref[...])
    mix = (jax.nn.sigmoid(ga_ref[...]) * _mm(ya.astype(BF16), wa_ref[...])
           + jax.nn.sigmoid(gb_ref[...]) * _mm(yb.astype(BF16), wb_ref[...]))
    o_ref[...] = h_ref[...] + _mm(mix.astype(BF16), wo_ref[...])


def _merge(h, ya_p, ya_s, yb_p, yb_s, proj, wa, wb, wo, tm):
    T, D = h.shape
    split = ya_p.shape[0] // tm
    rowblk = lambda j: pl.BlockSpec((tm, D), lambda i: (i, j))
    pblk = pl.BlockSpec((tm, D), lambda i: (jnp.minimum(i, split - 1), 0))
    sblk = pl.BlockSpec((tm, D), lambda i: (jnp.maximum(i - split, 0), 0))
    wspec = pl.BlockSpec((D, D), lambda i: (0, 0))
    return pl.pallas_call(
        functools.partial(_merge_kernel, split=split),
        grid=(T // tm,),
        in_specs=[rowblk(0), pblk, sblk, pblk, sblk, rowblk(5), rowblk(6), wspec, wspec, wspec],
        out_specs=rowblk(0),
        out_shape=jax.ShapeDtypeStruct((T, D), F32),
        compiler_params=_params(("parallel",)),
        name="merge",
    )(h, ya_p, ya_s, yb_p, yb_s, proj, proj, wa, wb, wo)


def _extract_topk(s_ref, rank_ref, vals_ref):
    rank_ref[...] = jnp.full(rank_ref.shape, float(PEER_TOPK), F32)
    n = s_ref.shape[0]

    def step(k, carry):
        s = s_ref[...]
        m = jnp.max(s, axis=0)
        idx = lax.broadcasted_iota(jnp.int32, s.shape, 0)
        first = jnp.min(jnp.where(s == m[None], idx, n), axis=0)
        hit = idx == first[None]
        rank_ref[...] = jnp.where(hit, jnp.asarray(k, F32), rank_ref[...])
        s_ref[...] = jnp.where(hit, -jnp.inf, s)
        vals_ref[k] = m
        return carry

    lax.fori_loop(0, PEER_TOPK, step, 0)


def _sort16_desc(x):
    x = list(x)
    n = len(x)
    k = 2
    while k <= n:
        j = k // 2
        while j >= 1:
            for i in range(n):
                l = i ^ j
                if l > i:
                    hi, lo = jnp.maximum(x[i], x[l]), jnp.minimum(x[i], x[l])
                    x[i], x[l] = (hi, lo) if (i & k) == 0 else (lo, hi)
            j //= 2
        k *= 2
    return x


def _merge_top16(a, b):
    n = len(a)
    x = [jnp.maximum(a[i], b[n - 1 - i]) for i in range(n)]
    j = n // 2
    while j >= 1:
        for i in range(n):
            l = i ^ j
            if l > i:
                x[i], x[l] = jnp.maximum(x[i], x[l]), jnp.minimum(x[i], x[l])
        j //= 2
    return x


def _top16_sorted(rows):
    groups = [_sort16_desc(rows[g:g + PEER_TOPK]) for g in range(0, len(rows), PEER_TOPK)]
    while len(groups) > 1:
        groups = [_merge_top16(groups[i], groups[i + 1]) for i in range(0, len(groups), 2)]
    return groups[0]


def _select_kernel(h_ref, g_ref, wq_ref, kb1_ref, kb2_ref,
                   xnt_ref, n_ref, e1_ref, r2_ref, e2_ref,
                   s1_scr, s2_scr, w_scr, rk1_scr, rk2_scr, v1_scr, v2_scr,
                   cw_scr, crk_scr, cv_scr, pk_scr, *, tb):
    H, K = PEER_HEADS, PEER_KEYS
    half = H * K
    xn = _rms(h_ref[...], g_ref[...])
    xnt_ref[...] = xn.T.astype(BF16)
    q = _mm(xn.astype(BF16), wq_ref[...]).astype(BF16)
    s1 = _nt(kb1_ref[...], q[:, :half])
    s2 = _nt(kb2_ref[...], q[:, half:])
    nlt = tb // LANES
    for lt in range(nlt):
        s1_scr[lt] = s1[:, lt * LANES:(lt + 1) * LANES].reshape(K, H, LANES)
        s2_scr[lt] = s2[:, lt * LANES:(lt + 1) * LANES].reshape(K, H, LANES)

    topk = float(PEER_TOPK)

    def by_value(lt):
        a = _top16_sorted([s1_scr[lt, k] for k in range(K)])
        b = _top16_sorted([s2_scr[lt, k] for k in range(K)])
        cands = [a[r1] + b[r2] for r1, r2 in CAND_PAIRS]
        pad = [jnp.full((H, LANES), -jnp.inf, F32)] * (-len(cands) % PEER_TOPK)
        t = _top16_sorted(cands + pad)
        won = [jnp.where(c >= t[-1], 1.0, 0.0) for c in cands]
        ncnt = [sum(w for w, (r1, _) in zip(won, CAND_PAIRS) if r1 == r) for r in range(PEER_TOPK)]
        tie = jnp.where(sum(won) != topk, 1.0, 0.0)
        for v in (a, b, t):
            for k in range(PEER_TOPK - 1):
                tie = jnp.where(v[k] == v[k + 1], 1.0, tie)
        rz = 1.0 / sum(jnp.exp(v - t[0]) for v in t)

        def keys0(i, cnt):
            s = s1_scr[lt, i]
            n_i = jnp.zeros_like(s)
            for r in range(PEER_TOPK):
                n_i = jnp.where(s == a[r], ncnt[r], n_i)
            rows = pl.ds(pl.multiple_of(i * H, H), H)
            pk_scr[0, rows, :] = n_i
            pk_scr[1, rows, :] = jnp.exp(s - a[0]) * rz
            return cnt + jnp.where(s >= a[-1], 1.0, 0.0)

        def keys1(j, cnt):
            s = s2_scr[lt, j]
            rank = jnp.full_like(s, topk)
            for r in range(PEER_TOPK):
                rank = jnp.where(s == b[r], float(r), rank)
            rows = pl.ds(pl.multiple_of(j * H, H), H)
            pk_scr[2, rows, :] = rank
            pk_scr[3, rows, :] = jnp.exp(s - b[0])
            return cnt + jnp.where(s >= b[-1], 1.0, 0.0)

        zero = jnp.zeros((H, LANES), F32)
        for body in (keys0, keys1):
            tie = jnp.where(lax.fori_loop(0, K, body, zero, unroll=8) != topk, 1.0, tie)
        return tie

    def by_extraction(lt):
        w_scr[...] = s1_scr[lt]
        _extract_topk(w_scr, rk1_scr, v1_scr)
        w_scr[...] = s2_scr[lt]
        _extract_topk(w_scr, rk2_scr, v2_scr)
        for p, (r1, r2) in enumerate(CAND_PAIRS):
            cw_scr[p] = v1_scr[r1] + v2_scr[r2]
        _extract_topk(cw_scr, crk_scr, cv_scr)
        top = cv_scr[0]
        zsum = jnp.zeros_like(top)
        for k in range(PEER_TOPK):
            zsum = zsum + jnp.exp(cv_scr[k] - top)
        rz = 1.0 / zsum
        rk1 = rk1_scr[...]
        n_i = jnp.zeros_like(rk1)
        for r1 in range(PEER_TOPK):
            cnt = jnp.zeros_like(top)
            for p, (a, _) in enumerate(CAND_PAIRS):
                if a == r1:
                    cnt = cnt + jnp.where(crk_scr[p] < topk, 1.0, 0.0)
            n_i = jnp.where(rk1 == float(r1), cnt[None], n_i)
        pk_scr[0] = n_i.reshape(half, LANES)
        pk_scr[1] = (jnp.exp(s1_scr[lt] - v1_scr[0][None]) * rz[None]).reshape(half, LANES)
        pk_scr[2] = rk2_scr[...].reshape(half, LANES)
        pk_scr[3] = jnp.exp(s2_scr[lt] - v2_scr[0][None]).reshape(half, LANES)

    def lane_tile(lt, carry):
        tie = by_value(lt)

        @pl.when(jnp.max(tie) > 0.0)
        def _():
            by_extraction(lt)

        lanes = pl.ds(pl.multiple_of(lt * LANES, LANES), LANES)
        for h in range(H):
            n_ref[h, :, lanes] = pk_scr[0, pl.ds(h, K, stride=H), :]
            e1_ref[h, :, lanes] = pk_scr[1, pl.ds(h, K, stride=H), :]
            for a, o_ref in ((2, r2_ref), (3, e2_ref)):
                lo = pk_scr[a, pl.ds(h, K // 2, stride=H), :]
                hi = pk_scr[a, pl.ds(h + (K // 2) * H, K // 2, stride=H), :]
                packed = _pack_pair(lo, hi)
                for jt in range(K // BF16_ROWS):
                    o_ref[lt, jt, h] = packed[jt * 8:(jt + 1) * 8]
        return carry

    lax.fori_loop(0, nlt, lane_tile, 0)


def _peer_select(h1, g, wq, kb1, kb2, tb):
    T, D = h1.shape
    H, K = PEER_HEADS, PEER_KEYS
    nlt = tb // LANES
    ncand = len(CAND_PAIRS)
    const = lambda shape: pl.BlockSpec(shape, lambda i: (0,) * len(shape))
    hkt = jax.ShapeDtypeStruct((H, K, T), F32)
    hkt_spec = pl.BlockSpec((H, K, tb), lambda i: (0, 0, i))
    pair = jax.ShapeDtypeStruct((T // LANES, K // BF16_ROWS, H, 8, LANES), PACKED)
    pair_spec = pl.BlockSpec((nlt, K // BF16_ROWS, H, 8, LANES), lambda i: (i, 0, 0, 0, 0))
    tile = lambda n: pltpu.VMEM((n, H, LANES), F32)
    return pl.pallas_call(
        functools.partial(_select_kernel, tb=tb),
        grid=(T // tb,),
        in_specs=[pl.BlockSpec((tb, D), lambda i: (i, 0)), const((1, D)), const(wq.shape),
                  const(kb1.shape), const(kb2.shape)],
        out_specs=[pl.BlockSpec((D, tb), lambda i: (0, i)), hkt_spec, hkt_spec, pair_spec, pair_spec],
        out_shape=[jax.ShapeDtypeStruct((D, T), BF16), hkt, hkt, pair, pair],
        scratch_shapes=[pltpu.VMEM((nlt, K, H, LANES), F32), pltpu.VMEM((nlt, K, H, LANES), F32),
                        tile(K), tile(K), tile(K), tile(PEER_TOPK), tile(PEER_TOPK),
                        tile(ncand), tile(ncand), tile(PEER_TOPK),
                        pltpu.VMEM((4, H * K, LANES), F32)],
        compiler_params=_params(("parallel",)),
        name="peer_select",
    )(h1, g, wq, kb1, kb2)


def _gelu_tanh(x):
    c = math.sqrt(2.0 / math.pi)
    hx = 0.5 * x
    return hx * jnp.tanh(x * (c + (c * 0.044715) * (x * x))) + hx


def _dense_kernel(h_ref, xnt_ref, n_ref, e1_ref, r2_ref, e2_ref, u_ref, vt_ref, o_ref,
                  pre0_scr, pre1_scr, act0_scr, act1_scr, acc_scr, *, tb, ec):
    k = pl.program_id(1)
    last = pl.num_programs(1) - 1
    H, K = PEER_HEADS, PEER_KEYS
    ni = ec // K
    zero = jnp.zeros((), BF16)
    lw = 2 * LANES

    def stage_b(cb, pre_ref, act_ref, lp):
        for lt in range(lw // LANES):
            ltile = lp * (lw // LANES) + lt
            lanes = pl.ds(pl.multiple_of(ltile * LANES, LANES), LANES)
            for i8 in range(ni // 8):
                irows = pl.ds(pl.multiple_of(cb * ni + i8 * 8, 8), 8)
                n8 = [n_ref[h, irows, lanes] for h in range(H)]
                e8 = [e1_ref[h, irows, lanes] for h in range(H)]
                for i1 in range(8):
                    il = i8 * 8 + i1
                    nb = [jnp.broadcast_to(n8[h][i1:i1 + 1, :], (BF16_ROWS, LANES)).astype(BF16) for h in range(H)]
                    eb = [jnp.broadcast_to(e8[h][i1:i1 + 1, :], (BF16_ROWS, LANES)).astype(BF16) for h in range(H)]
                    for jt in range(K // BF16_ROWS):
                        gate = None
                        for h in range(H):
                            r = pltpu.bitcast(r2_ref[ltile, jt, h], BF16)
                            e = pltpu.bitcast(e2_ref[ltile, jt, h], BF16)
                            t = jnp.where(r < nb[h], e, zero) * eb[h]
                            gate = t if gate is None else gate + t
                        gate = pltpu.bitcast(gate, PACKED)
                        for half in range(2):
                            g = _unpack_pair(gate, half)
                            r0 = il * K + half * (K // 2) + jt * 8
                            act_ref[r0:r0 + 8, lanes] = (_gelu_tanh(pre_ref[r0:r0 + 8, lanes]) * g).astype(BF16)

    def sub_step(sub, do_a, do_b, do_c):
        s = 2 * k + sub
        first = s == 2
        pre_w, pre_r = (pre0_scr, pre1_scr) if sub == 0 else (pre1_scr, pre0_scr)
        act_r, act_w = (act0_scr, act1_scr) if sub == 0 else (act1_scr, act0_scr)

        def body(lp, carry):
            lanes = pl.ds(pl.multiple_of(lp * lw, lw), lw)
            if do_a:
                u = pltpu.bitcast(u_ref[sub * (ec // 2):(sub + 1) * (ec // 2), :], BF16)
                pre_w[:, lanes] = _mm(u, xnt_ref[:, lanes])
            if do_b:
                stage_b(s - 1, pre_r, act_w, lp)
            if do_c:
                vt = pltpu.bitcast(vt_ref[:, sub * ec:(sub + 1) * ec], BF16)
                contrib = _mm(vt, act_r[:, lanes])
                acc_scr[:, lanes] = jnp.where(first, contrib, acc_scr[:, lanes] + contrib)
            return carry

        lax.fori_loop(0, tb // lw, body, 0)

    @pl.when(k == 0)
    def _():
        sub_step(0, True, False, False)
        sub_step(1, True, True, False)

    @pl.when((k > 0) & (k < last))
    def _():
        sub_step(0, True, True, True)
        sub_step(1, True, True, True)

    @pl.when(k == last)
    def _():
        sub_step(0, False, True, True)
        sub_step(1, False, False, True)

    @pl.when(k == last)
    def _():
        o_ref[...] = h_ref[...] + acc_scr[...].T


def _pack_kernel(w_ref, o_ref, *, transpose):
    w = w_ref[...]
    o_ref[...] = pltpu.bitcast((w.T if transpose else w).astype(BF16), PACKED)


def _pack_table(w, layer, *, transpose, rows=1024):
    _, E, D = w.shape
    if transpose:
        out_shape, out_spec = (D // 2, E), pl.BlockSpec((D // 2, rows), lambda i: (0, i))
    else:
        out_shape, out_spec = (E // 2, D), pl.BlockSpec((rows // 2, D), lambda i: (i, 0))
    return pl.pallas_call(
        functools.partial(_pack_kernel, transpose=transpose),
        grid=(E // rows,),
        in_specs=[pl.BlockSpec((None, rows, D), lambda i: (layer, i, 0))],
        out_specs=out_spec,
        out_shape=jax.ShapeDtypeStruct(out_shape, PACKED),
        compiler_params=_params(("parallel",)),
        name="pack_table_t" if transpose else "pack_table",
    )(w)


def _peer_dense(h1, xnt, n, e1, r2, e2, u, vt, tb, ec):
    T, D = h1.shape
    H, K = PEER_HEADS, PEER_KEYS
    nc = 2 * u.shape[0] // ec
    assert nc % 2 == 0 and nc >= 4
    nsteps = nc // 2 + 1
    hkt_spec = pl.BlockSpec((H, K, tb), lambda i, k: (0, 0, i))
    pair_spec = pl.BlockSpec((tb // LANES, K // BF16_ROWS, H, 8, LANES), lambda i, k: (i, 0, 0, 0, 0))
    return pl.pallas_call(
        functools.partial(_dense_kernel, tb=tb, ec=ec),
        grid=(T // tb, nsteps),
        in_specs=[pl.BlockSpec((tb, D), lambda i, k: (i, 0)),
                  pl.BlockSpec((D, tb), lambda i, k: (0, i)),
                  hkt_spec, hkt_spec, pair_spec, pair_spec,
                  pl.BlockSpec((ec, D), lambda i, k: (jnp.minimum(k, nsteps - 2), 0)),
                  pl.BlockSpec((D // 2, 2 * ec), lambda i, k: (0, jnp.maximum(k - 1, 0)))],
        out_specs=pl.BlockSpec((tb, D), lambda i, k: (i, 0)),
        out_shape=jax.ShapeDtypeStruct((T, D), F32),
        scratch_shapes=[pltpu.VMEM((ec, tb), F32), pltpu.VMEM((ec, tb), F32),
                        pltpu.VMEM((ec, tb), BF16), pltpu.VMEM((ec, tb), BF16), pltpu.VMEM((D, tb), F32)],
        compiler_params=_params(("arbitrary", "arbitrary")),
        name="peer_dense",
    )(h1, xnt, n, e1, r2, e2, u, vt)


def _ple_kernel(h_ref, p_ref, g_ref, wg_ref, wp_ref, gf_ref, *o_refs, split):
    h = h_ref[...]
    gate = jax.nn.sigmoid(_mm(_rms(h, g_ref[...]).astype(BF16), wg_ref[...]))
    h = h + gate * _mm(p_ref[...].astype(BF16), wp_ref[...])
    if split is None:
        o_refs[0][...] = h
        return
    y = _rms(h, gf_ref[...])
    o_refs[1][...] = y

    @pl.when(pl.program_id(0) < split)
    def _():
        o_refs[0][...] = y


def _ple(h, p, g, wg, wp, gf, tm, split_rows=None):
    T, D = h.shape
    P = p.shape[1]
    const = lambda shape: pl.BlockSpec(shape, lambda i: (0,) * len(shape))
    tile = lambda fn: pl.BlockSpec((tm, D), fn)
    if split_rows is None:
        split, out_specs, out_shape = None, tile(lambda i: (i, 0)), jax.ShapeDtypeStruct((T, D), F32)
    else:
        split = split_rows // tm
        out_specs = [tile(lambda i: (jnp.minimum(i, split - 1), 0)), tile(lambda i: (jnp.maximum(i - split, 0), 0))]
        out_shape = [jax.ShapeDtypeStruct((split_rows, D), F32), jax.ShapeDtypeStruct((T - split_rows, D), F32)]
    return pl.pallas_call(
        functools.partial(_ple_kernel, split=split),
        grid=(T // tm,),
        in_specs=[tile(lambda i: (i, 0)), pl.BlockSpec((tm, P), lambda i: (i, 0)),
                  const((1, D)), const((D, D)), const((P, D)), const((1, D))],
        out_specs=out_specs,
        out_shape=out_shape,
        compiler_params=_params(("arbitrary",)),
        name="ple",
    )(h, p, g, wg, wp, gf)


def _tiles(tp, ts):
    tm = next(t for t in (512, 256, 128, 64, 32, 16, 8) if tp % t == 0 and ts % t == 0)
    tb = next(t for t in (512, 256) if (tp + ts) % t == 0)
    return tm, tb


def _pad_lanes(v, fill=0.0):
    return jnp.pad(v.astype(F32), (0, LANES - v.shape[0]), constant_values=fill)[None, :]


def kernel(x_prompt, x_sample, state_ssd, state_ssd_conv, state_cf_conv, p_prompt, p_sample, g_mix, w_in, ssd_conv_w, ssd_conv_b, ssd_dt_bias, ssd_a_log, ssd_d, ssd_norm_g, w_ssd_out, cf_dw_w, cf_dw_b, cf_ln_g, cf_ln_b, w_cf_out, w_o, g_ffn, peer_wq, peer_keys, peer_u, peer_v, g_ple, w_ple_gate, w_ple_proj, g_final):
    depth = w_in.shape[0]
    bp, lp, D = x_prompt.shape
    bs, ls, _ = x_sample.shape
    tp, ts = bp * lp, bs * ls
    T = tp + ts
    inner = SSD_HEADS * SSD_HEAD_DIM
    cd = state_ssd_conv.shape[-1]
    H, K = PEER_HEADS, PEER_KEYS
    tm, tb = _tiles(tp, ts)
    lc_p, lc_s = min(SSD_CHUNK, lp), min(SSD_CHUNK, ls)
    assert lp % lc_p == 0 and ls % lc_s == 0 and tp % lc_s == 0 and D == inner == cf_dw_w.shape[-1]

    h = jnp.concatenate([x_prompt.reshape(tp, D), x_sample.reshape(ts, D)], axis=0)
    expand = (jnp.arange(inner)[None, :] // SSD_HEAD_DIM == jnp.arange(LANES)[:, None]).astype(BF16)
    seqs_per_step = next(n for n in (8, 4, 2, 1) if bs % n == 0 and tp % (n * lc_s) == 0) if ls == lc_s else 1
    zeros_conv = jnp.zeros((1, bp, SSD_CONV - 1, cd), F32)
    zeros_h = jnp.zeros((1, bp, SSD_HEADS, SSD_HEAD_DIM, SSD_STATE), F32)
    zeros_cf = jnp.zeros((1, bp, CF_CONV - 1, D), F32)
    row = lambda v: v.astype(F32)[None, :]
    eye_h = jnp.eye(H, dtype=F32)
    outs = {k: [] for k in ("ssd_p", "sconv_p", "cf_p", "ssd_s", "sconv_s", "cf_s")}

    for i in range(depth):
        w = w_in[i]
        o_xbc, o_dt = inner, inner + cd
        o_glu = o_dt + SSD_HEADS
        w_r = jnp.concatenate([w[:, :o_xbc], w[:, o_xbc:o_dt], w[:, o_glu:],
                               jnp.pad(w[:, o_dt:o_glu], ((0, 0), (0, LANES - SSD_HEADS)))], axis=1).astype(BF16)
        ncols = w_r.shape[1]
        tn = next(t for t in (ncols // 3, ncols) if t % LANES == 0 and ncols % t == 0)
        proj = _inproj(h, row(g_mix[i]), w_r, tm, tn)

        ssd_w = (ssd_conv_w[i], row(ssd_conv_b[i]), _pad_lanes(ssd_dt_bias[i]), _pad_lanes(ssd_a_log[i]),
                 row(jnp.repeat(ssd_d[i], SSD_HEAD_DIM)), row(ssd_norm_g[i]), expand, expand.T)
        ya_p, sconv_p, ssd_p = _ssd(proj, zeros_conv, zeros_h, 0, ssd_w, nb=bp, nc=lp // lc_p, L=lc_p, row_off=0,
                                    nseq=1)
        ya_s, sconv_s, ssd_s = _ssd(proj, state_ssd_conv, state_ssd, i, ssd_w,
                                    nb=bs, nc=ls // lc_s, L=lc_s, row_off=tp, nseq=seqs_per_step)
        cf_w = (cf_dw_w[i], row(cf_dw_b[i]), row(cf_ln_g[i]), row(cf_ln_b[i]))
        yb_p, cf_p = _conformer(proj, zeros_cf, 0, cf_w, nb=bp, nc=lp // lc_p, L=lc_p, row_off=0, nseq=1)
        yb_s, cf_s = _conformer(proj, state_cf_conv, i, cf_w, nb=bs, nc=ls // lc_s, L=lc_s, row_off=tp,
                                nseq=seqs_per_step)
        for k, v in (("ssd_p", ssd_p), ("sconv_p", sconv_p), ("cf_p", cf_p),
                     ("ssd_s", ssd_s), ("sconv_s", sconv_s), ("cf_s", cf_s)):
            outs[k].append(v)
        h = _merge(h, ya_p, ya_s, yb_p, yb_s, proj,
                   w_ssd_out[i].astype(BF16), w_cf_out[i].astype(BF16), w_o[i].astype(BF16), tm)

        qd = peer_keys.shape[-1]
        wq = peer_wq[i].reshape(D, H, 2, qd).transpose(0, 2, 1, 3).reshape(D, 2 * H * qd).astype(BF16)
        kb = [jnp.einsum("hkd,hg->khgd", peer_keys[i, :, s], eye_h).reshape(K * H, H * qd).astype(BF16)
              for s in range(2)]
        xnt, n_sel, e1, r2, e2 = _peer_select(h, row(g_ffn[i]), wq, kb[0], kb[1], tb)
        h = _peer_dense(h, xnt, n_sel, e1, r2, e2, _pack_table(peer_u, i, transpose=False), _pack_table(peer_v, i, transpose=True),
                        tb, ec=8 * K)

        p_all = jnp.concatenate([p_prompt[i].reshape(tp, -1), p_sample[i].reshape(ts, -1)], axis=0)
        h = _ple(h, p_all, row(g_ple[i]), w_ple_gate[i].astype(BF16), w_ple_proj[i].astype(BF16),
                 row(g_final), tm, split_rows=tp if i == depth - 1 else None)

    y_prompt, y_sample = h
    st = {k: jnp.stack(v) for k, v in outs.items()}
    return (y_prompt.reshape(bp, lp, D), y_sample.reshape(bs, ls, D),
            st["ssd_p"], st["sconv_p"], st["cf_p"], st["ssd_s"], st["sconv_s"], st["cf_s"])
```

```python
import functools
import math

import jax
import jax.numpy as jnp
from jax import lax
from jax.experimental import pallas as pl
from jax.experimental.pallas import tpu as pltpu

F32, BF16 = jnp.float32, jnp.bfloat16
PACKED = jnp.uint32
EPS = 1e-6

SSD_HEADS, SSD_HEAD_DIM, SSD_GROUPS, SSD_STATE, SSD_CONV, SSD_CHUNK = 16, 64, 4, 128, 4, 128
CF_CONV = 31
PEER_HEADS, PEER_KEYS, PEER_TOPK = 8, 128, 16
LANES = 128
BF16_ROWS = 16
VMEM_LIMIT = 56 * 1024 * 1024

CAND_PAIRS = [(r1, r2) for r1 in range(PEER_TOPK) for r2 in range(PEER_TOPK)
              if (r1 + 1) * (r2 + 1) <= PEER_TOPK]


def _nt(a, b, **kw):
    return lax.dot_general(a, b, (((1,), (1,)), ((), ())), preferred_element_type=F32, **kw)


def _tn(a, b, **kw):
    return lax.dot_general(a, b, (((0,), (0,)), ((), ())), preferred_element_type=F32, **kw)


def _mm(a, b, **kw):
    return jnp.dot(a, b, preferred_element_type=F32, **kw)


def _rms(x, g):
    return x * lax.rsqrt(jnp.mean(x * x, axis=-1, keepdims=True) + EPS) * g


def _silu(x):
    return x * jax.nn.sigmoid(x)


def _pack_pair(lo, hi):
    bits = lambda v: lax.bitcast_convert_type(v.astype(BF16).astype(F32), PACKED)
    return (bits(hi) & jnp.uint32(0xFFFF0000)) | lax.shift_right_logical(bits(lo), jnp.uint32(16))


def _unpack_pair(packed, half):
    bits = (packed & jnp.uint32(0xFFFF0000)) if half else lax.shift_left(packed, jnp.uint32(16))
    return lax.bitcast_convert_type(bits, F32)


def _params(sem, flags=None):
    return pltpu.CompilerParams(dimension_semantics=sem, vmem_limit_bytes=VMEM_LIMIT, flags=flags)


def _inproj_kernel(x_ref, g_ref, w_ref, o_ref):
    o_ref[...] = _mm(_rms(x_ref[...], g_ref[...]).astype(BF16), w_ref[...])


def _inproj(h, g, w, tm, tn):
    T, D = h.shape
    N = w.shape[1]
    return pl.pallas_call(
        _inproj_kernel,
        grid=(N // tn, T // tm),
        in_specs=[pl.BlockSpec((tm, D), lambda j, i: (i, 0)),
                  pl.BlockSpec((1, D), lambda j, i: (0, 0)),
                  pl.BlockSpec((D, tn), lambda j, i: (0, j))],
        out_specs=pl.BlockSpec((tm, tn), lambda j, i: (i, j)),
        out_shape=jax.ShapeDtypeStruct((T, N), F32),
        compiler_params=_params(("parallel", "parallel")),
        name="inproj",
    )(h, g, w)


def _split3(x):
    x1 = x.astype(BF16)
    r1 = x - x1.astype(F32)
    x2 = r1.astype(BF16)
    x3 = (r1 - x2.astype(F32)).astype(BF16)
    return x1, x2, x3


def _mm_sel(x, sel):
    x1, x2, x3 = _split3(x)
    return (_mm(x3, sel) + _mm(x2, sel)) + _mm(x1, sel)


def _sel_mm(sel, x):
    x1, x2, x3 = _split3(x)
    return (_mm(sel, x3) + _mm(sel, x2)) + _mm(sel, x1)


def _sel_nt(sel, x):
    x1, x2, x3 = _split3(x)
    return (_nt(sel, x3) + _nt(sel, x2)) + _nt(sel, x1)


def _ssd_kernel(z_ref, xs_ref, bc_ref, dt_ref, conv0_ref, h0_ref, cw_ref, cb_ref, dtb_ref, alog_ref,
                dexp_ref, ng_ref, e_ref, et_ref,
                y_ref, nconv_ref, ht_ref,
                xpad_ref, st_ref, yscr_ref, *, L, nseq):
    c = pl.program_id(1)
    inner = SSD_HEADS * SSD_HEAD_DIM
    hist = SSD_CONV - 1
    base = 8
    gn = SSD_GROUPS * SSD_STATE
    gw = inner // SSD_GROUPS
    heads_per_group = SSD_HEADS // SSD_GROUPS

    @pl.when(c == 0)
    def _():
        xpad_ref[:, base - hist:base, :] = conv0_ref[...]
        st_ref[...] = h0_ref[...]

    row = lax.broadcasted_iota(jnp.int32, (L, L), 0)
    col = lax.broadcasted_iota(jnp.int32, (L, L), 1)
    causal = row >= col
    tri = jnp.where(causal, 1.0, 0.0).astype(BF16)
    lane = lax.broadcasted_iota(jnp.int32, (L, 2 * SSD_HEAD_DIM), 1)
    a_neg = -jnp.exp(alog_ref[...])
    e = e_ref[...]

    for q in range(nseq):
        rows = slice(q * L, (q + 1) * L)
        xpad_ref[q, base:base + L, 0:inner] = xs_ref[rows, :]
        xpad_ref[q, base:base + L, inner:] = bc_ref[rows, :]

        conv = cb_ref[...] + cw_ref[0:1, :] * xpad_ref[q, base - hist:base - hist + L, :]
        for k in range(1, SSD_CONV):
            conv = conv + cw_ref[k:k + 1, :] * xpad_ref[q, base - hist + k:base - hist + k + L, :]
        last_rows = xpad_ref[q, base + L - hist:base + L, :]
        nconv_ref[q] = last_rows
        xpad_ref[q, base - hist:base, :] = last_rows
        xc = _silu(conv)
        xs = xc[:, :inner]
        bm = xc[:, inner:inner + gn].astype(BF16)
        cm = xc[:, inner + gn:].astype(BF16)

        x = dt_ref[rows, :] + dtb_ref[...]
        dt = jnp.maximum(x, 0.0) + jnp.log1p(jnp.exp(-jnp.abs(x)))
        acs = _sel_mm(tri, dt * a_neg)
        dt_x = _mm_sel(dt, e)
        acs_x = _mm_sel(acs, e)
        acs_t = _sel_nt(et_ref[...], acs)
        alast_x = acs_x[L - 1:L, :]
        xd = xs * dt_x
        xd_bf = xd.astype(BF16)
        xdec = (xd * jnp.exp(alast_x - acs_x)).astype(BF16)
        eacs = jnp.exp(acs_x)

        for g in range(SSD_GROUPS):
            bg = bm[:, g * SSD_STATE:(g + 1) * SSD_STATE]
            cg = cm[:, g * SSD_STATE:(g + 1) * SSD_STATE]
            cb = _nt(cg, bg)
            for pp in range(heads_per_group // 2):
                p = g * (heads_per_group // 2) + pp
                cols = slice(p * 2 * SSD_HEAD_DIM, (p + 1) * 2 * SSD_HEAD_DIM)
                xp = xd_bf[:, cols]
                ys = []
                for hh in range(2):
                    h = 2 * p + hh
                    acol = acs[:, h:h + 1]
                    arow = acs_t[h * SSD_HEAD_DIM:h * SSD_HEAD_DIM + 1, :]
                    lm = jnp.where(causal, jnp.exp(acol - arow), 0.0)
                    ys.append(_mm((cb * lm).astype(BF16), xp))
                ydiag = jnp.where(lane < SSD_HEAD_DIM, ys[0], ys[1])
                st = st_ref[q, p]
                yoff = _nt(cg, st.astype(BF16)) * eacs[:, cols]
                yscr_ref[rows, cols] = ydiag + yoff
                scale = jnp.exp(acs_t[p * 2 * SSD_HEAD_DIM:(p + 1) * 2 * SSD_HEAD_DIM, L - 1:L])
                st_ref[q, p] = st * scale + _tn(xdec[:, cols], bg)

        y = (yscr_ref[rows, :] + dexp_ref[...] * xs) * _silu(z_ref[rows, :])
        for g in range(SSD_GROUPS):
            seg = y[:, g * gw:(g + 1) * gw]
            seg = seg * lax.rsqrt(jnp.mean(seg * seg, axis=-1, keepdims=True) + EPS)
            y_ref[rows, g * gw:(g + 1) * gw] = seg * ng_ref[:, g * gw:(g + 1) * gw]

    ht_ref[...] = st_ref[...]


def _ssd(proj, conv0, h0, layer, wts, *, nb, nc, L, row_off, nseq):
    assert nb % nseq == 0 and (nseq == 1 or nc == 1) and row_off % (nseq * L) == 0
    inner = SSD_HEADS * SSD_HEAD_DIM
    cd = conv0.shape[-1]
    R = nseq * L
    r0 = row_off // R
    rows = lambda b, c: r0 + b * nc + c
    dt_blk = (proj.shape[1] - LANES) // LANES
    const = lambda shape: pl.BlockSpec(shape, lambda b, c: (0,) * len(shape))
    npair = SSD_HEADS // 2
    y, nconv, ht = pl.pallas_call(
        functools.partial(_ssd_kernel, L=L, nseq=nseq),
        grid=(nb // nseq, nc),
        in_specs=[pl.BlockSpec((R, inner), lambda b, c: (rows(b, c), 0)),
                  pl.BlockSpec((R, inner), lambda b, c: (rows(b, c), 1)),
                  pl.BlockSpec((R, inner), lambda b, c: (rows(b, c), 2)),
                  pl.BlockSpec((R, LANES), lambda b, c: (rows(b, c), dt_blk)),
                  pl.BlockSpec((None, nseq, SSD_CONV - 1, cd), lambda b, c: (layer, b, 0, 0)),
                  pl.BlockSpec((None, nseq, npair, LANES, SSD_STATE), lambda b, c: (layer, b, 0, 0, 0)),
                  const((SSD_CONV, cd)), const((1, cd)), const((1, LANES)), const((1, LANES)),
                  const((1, inner)), const((1, inner)), const((LANES, inner)), const((inner, LANES))],
        out_specs=[pl.BlockSpec((R, inner), lambda b, c: (b * nc + c, 0)),
                   pl.BlockSpec((nseq, SSD_CONV - 1, cd), lambda b, c: (b, 0, 0)),
                   pl.BlockSpec((nseq, npair, LANES, SSD_STATE), lambda b, c: (b, 0, 0, 0))],
        out_shape=[jax.ShapeDtypeStruct((nb * nc * L, inner), F32),
                   jax.ShapeDtypeStruct((nb, SSD_CONV - 1, cd), F32),
                   jax.ShapeDtypeStruct((nb, npair, LANES, SSD_STATE), F32)],
        scratch_shapes=[pltpu.VMEM((nseq, 8 + L, cd), F32),
                        pltpu.VMEM((nseq, npair, LANES, SSD_STATE), F32),
                        pltpu.VMEM((R, inner), F32)],
        compiler_params=_params(("parallel", "arbitrary")),
        name=f"ssd_L{L}",
    )(proj, proj, proj, proj, conv0, h0.reshape(-1, nb, npair, LANES, SSD_STATE), *wts)
    return y, nconv, ht.reshape(nb, SSD_HEADS, SSD_HEAD_DIM, SSD_STATE)


def _cf_kernel(a_ref, g_ref, buf0_ref, w_ref, b_ref, lng_ref, lnb_ref, y_ref, nbuf_ref,
               upad_ref, shift_ref, conv_ref, *, L, nseq):
    c = pl.program_id(1)
    hist = CF_CONV - 1
    base = 32
    dim = a_ref.shape[1]

    @pl.when(c == 0)
    def _():
        upad_ref[:, base - hist:base, :] = buf0_ref[...]

    first = base - hist
    span = shift_ref.shape[2]
    slab = 2 * LANES
    for q in range(nseq):
        rows = slice(q * L, (q + 1) * L)
        upad_ref[q, base:base + L, :] = a_ref[rows, :] * jax.nn.sigmoid(g_ref[rows, :])
        for s in range(dim // slab):
            cols = slice(s * slab, (s + 1) * slab)
            acc = jnp.broadcast_to(b_ref[:, cols], (L, slab))
            for phase in range(8):
                if phase:
                    shift_ref[q, phase - 1, :, cols] = upad_ref[q, phase:phase + span, cols]
                for k in range(CF_CONV):
                    if (first + k) % 8 == phase:
                        r0 = first + k - phase
                        src = (upad_ref[q, r0:r0 + L, cols] if phase == 0
                               else shift_ref[q, phase - 1, r0:r0 + L, cols])
                        acc = acc + w_ref[k:k + 1, cols] * src
            conv_ref[rows, cols] = acc
        nbuf = upad_ref[q, base + L - hist:base + L, :]
        nbuf_ref[q] = nbuf
        upad_ref[q, base - hist:base, :] = nbuf

    x = conv_ref[...]
    xc = x - jnp.mean(x, axis=-1, keepdims=True)
    y = xc * lax.rsqrt(jnp.mean(xc * xc, axis=-1, keepdims=True) + EPS) * lng_ref[...] + lnb_ref[...]
    y_ref[...] = _silu(y)


def _conformer(proj, buf0, layer, wts, *, nb, nc, L, row_off, nseq):
    assert nb % nseq == 0 and (nseq == 1 or nc == 1) and row_off % (nseq * L) == 0
    dim = buf0.shape[-1]
    R = nseq * L
    r0 = row_off // R
    rows = lambda b, c: r0 + b * nc + c
    const = lambda shape: pl.BlockSpec(shape, lambda b, c: (0,) * len(shape))
    return pl.pallas_call(
        functools.partial(_cf_kernel, L=L, nseq=nseq),
        grid=(nb // nseq, nc),
        in_specs=[pl.BlockSpec((R, dim), lambda b, c: (rows(b, c), 3)),
                  pl.BlockSpec((R, dim), lambda b, c: (rows(b, c), 4)),
                  pl.BlockSpec((None, nseq, CF_CONV - 1, dim), lambda b, c: (layer, b, 0, 0)),
                  const((CF_CONV, dim)), const((1, dim)), const((1, dim)), const((1, dim))],
        out_specs=[pl.BlockSpec((R, dim), lambda b, c: (b * nc + c, 0)),
                   pl.BlockSpec((nseq, CF_CONV - 1, dim), lambda b, c: (b, 0, 0))],
        out_shape=[jax.ShapeDtypeStruct((nb * nc * L, dim), F32),
                   jax.ShapeDtypeStruct((nb, CF_CONV - 1, dim), F32)],
        scratch_shapes=[pltpu.VMEM((nseq, 32 + L, dim), F32), pltpu.VMEM((nseq, 7, 24 + L, dim), F32),
                        pltpu.VMEM((R, dim), F32)],
        compiler_params=_params(("parallel", "arbitrary")),
        name=f"conformer_L{L}",
    )(proj, proj, buf0, *wts)


def _merge_kernel(h_ref, yap_ref, yas_ref, ybp_ref, ybs_ref, ga_ref, gb_ref, wa_ref, wb_ref, wo_ref, o_ref,
                  *, split):
    prompt = pl.program_id(0) < split
    ya = jnp.where(prompt, yap_ref[...], yas_ref[...])
    yb = jnp.where(prompt, ybp_ref[...], ybs_ref[...])
    mix = (jax.nn.sigmoid(ga_ref[...]) * _mm(ya.astype(BF16), wa_ref[...])
           + jax.nn.sigmoid(gb_ref[...]) * _mm(yb.astype(BF16), wb_ref[...]))
    o_ref[...] = h_ref[...] + _mm(mix.astype(BF16), wo_ref[...])


def _merge(h, ya_p, ya_s, yb_p, yb_s, proj, wa, wb, wo, tm):
    T, D = h.shape
    split = ya_p.shape[0] // tm
    rowblk = lambda j: pl.BlockSpec((tm, D), lambda i: (i, j))
    pblk = pl.BlockSpec((tm, D), lambda i: (jnp.minimum(i, split - 1), 0))
    sblk = pl.BlockSpec((tm, D), lambda i: (jnp.maximum(i - split, 0), 0))
    wspec = pl.BlockSpec((D, D), lambda i: (0, 0))
    return pl.pallas_call(
        functools.partial(_merge_kernel, split=split),
        grid=(T // tm,),
        in_specs=[rowblk(0), pblk, sblk, pblk, sblk, rowblk(5), rowblk(6), wspec, wspec, wspec],
        out_specs=rowblk(0),
        out_shape=jax.ShapeDtypeStruct((T, D), F32),
        compiler_params=_params(("parallel",)),
        name="merge",
    )(h, ya_p, ya_s, yb_p, yb_s, proj, proj, wa, wb, wo)


def _extract_topk(s_ref, rank_ref, vals_ref):
    rank_ref[...] = jnp.full(rank_ref.shape, float(PEER_TOPK), F32)
    n = s_ref.shape[0]

    def step(k, carry):
        s = s_ref[...]
        m = jnp.max(s, axis=0)
        idx = lax.broadcasted_iota(jnp.int32, s.shape, 0)
        first = jnp.min(jnp.where(s == m[None], idx, n), axis=0)
        hit = idx == first[None]
        rank_ref[...] = jnp.where(hit, jnp.asarray(k, F32), rank_ref[...])
        s_ref[...] = jnp.where(hit, -jnp.inf, s)
        vals_ref[k] = m
        return carry

    lax.fori_loop(0, PEER_TOPK, step, 0)


def _sort16_desc(x):
    x = list(x)
    n = len(x)
    k = 2
    while k <= n:
        j = k // 2
        while j >= 1:
            for i in range(n):
                l = i ^ j
                if l > i:
                    hi, lo = jnp.maximum(x[i], x[l]), jnp.minimum(x[i], x[l])
                    x[i], x[l] = (hi, lo) if (i & k) == 0 else (lo, hi)
            j //= 2
        k *= 2
    return x


def _merge_top16(a, b):
    n = len(a)
    x = [jnp.maximum(a[i], b[n - 1 - i]) for i in range(n)]
    j = n // 2
    while j >= 1:
        for i in range(n):
            l = i ^ j
            if l > i:
                x[i], x[l] = jnp.maximum(x[i], x[l]), jnp.minimum(x[i], x[l])
        j //= 2
    return x


def _top16_sorted(rows):
    groups = [_sort16_desc(rows[g:g + PEER_TOPK]) for g in range(0, len(rows), PEER_TOPK)]
    while len(groups) > 1:
        groups = [_merge_top16(groups[i], groups[i + 1]) for i in range(0, len(groups), 2)]
    return groups[0]


def _select_kernel(h_ref, g_ref, wq_ref, kb1_ref, kb2_ref,
                   xnt_ref, n_ref, e1_ref, r2_ref, e2_ref,
                   s1_scr, s2_scr, w_scr, rk1_scr, rk2_scr, v1_scr, v2_scr,
                   cw_scr, crk_scr, cv_scr, pk_scr, *, tb):
    H, K = PEER_HEADS, PEER_KEYS
    half = H * K
    xn = _rms(h_ref[...], g_ref[...])
    xnt_ref[...] = xn.T.astype(BF16)
    q = _mm(xn.astype(BF16), wq_ref[...]).astype(BF16)
    s1 = _nt(kb1_ref[...], q[:, :half])
    s2 = _nt(kb2_ref[...], q[:, half:])
    nlt = tb // LANES
    for lt in range(nlt):
        s1_scr[lt] = s1[:, lt * LANES:(lt + 1) * LANES].reshape(K, H, LANES)
        s2_scr[lt] = s2[:, lt * LANES:(lt + 1) * LANES].reshape(K, H, LANES)

    topk = float(PEER_TOPK)

    def by_value(lt):
        a = _top16_sorted([s1_scr[lt, k] for k in range(K)])
        b = _top16_sorted([s2_scr[lt, k] for k in range(K)])
        cands = [a[r1] + b[r2] for r1, r2 in CAND_PAIRS]
        pad = [jnp.full((H, LANES), -jnp.inf, F32)] * (-len(cands) % PEER_TOPK)
        t = _top16_sorted(cands + pad)
        won = [jnp.where(c >= t[-1], 1.0, 0.0) for c in cands]
        ncnt = [sum(w for w, (r1, _) in zip(won, CAND_PAIRS) if r1 == r) for r in range(PEER_TOPK)]
        tie = jnp.where(sum(won) != topk, 1.0, 0.0)
        for v in (a, b, t):
            for k in range(PEER_TOPK - 1):
                tie = jnp.where(v[k] == v[k + 1], 1.0, tie)
        rz = 1.0 / sum(jnp.exp(v - t[0]) for v in t)

        def keys0(i, cnt):
            s = s1_scr[lt, i]
            n_i = jnp.zeros_like(s)
            for r in range(PEER_TOPK):
                n_i = jnp.where(s == a[r], ncnt[r], n_i)
            rows = pl.ds(pl.multiple_of(i * H, H), H)
            pk_scr[0, rows, :] = n_i
            pk_scr[1, rows, :] = jnp.exp(s - a[0]) * rz
            return cnt + jnp.where(s >= a[-1], 1.0, 0.0)

        def keys1(j, cnt):
            s = s2_scr[lt, j]
            c8 = b[7] > s
            c4 = jnp.where(c8, b[11], b[3]) > s
            c2 = jnp.where(c8, jnp.where(c4, b[13], b[9]), jnp.where(c4, b[5], b[1])) > s
            hi = jnp.where(c4, jnp.where(c2, b[14], b[12]), jnp.where(c2, b[10], b[8]))
            lo = jnp.where(c4, jnp.where(c2, b[6], b[4]), jnp.where(c2, b[2], b[0]))
            c1 = jnp.where(c8, hi, lo) > s
            rank = ((jnp.where(c8, 8.0, 0.0) + jnp.where(c4, 4.0, 0.0))
                    + (jnp.where(c2, 2.0, 0.0) + jnp.where(c1, 1.0, 0.0)) + jnp.where(b[15] > s, 1.0, 0.0))
            rows = pl.ds(pl.multiple_of(j * H, H), H)
            pk_scr[2, rows, :] = rank
            pk_scr[3, rows, :] = jnp.exp(s - b[0])
            return cnt + jnp.where(s >= b[-1], 1.0, 0.0)

        zero = jnp.zeros((H, LANES), F32)
        for body in (keys0, keys1):
            tie = jnp.where(lax.fori_loop(0, K, body, zero, unroll=8) != topk, 1.0, tie)
        return tie

    def by_extraction(lt):
        w_scr[...] = s1_scr[lt]
        _extract_topk(w_scr, rk1_scr, v1_scr)
        w_scr[...] = s2_scr[lt]
        _extract_topk(w_scr, rk2_scr, v2_scr)
        for p, (r1, r2) in enumerate(CAND_PAIRS):
            cw_scr[p] = v1_scr[r1] + v2_scr[r2]
        _extract_topk(cw_scr, crk_scr, cv_scr)
        top = cv_scr[0]
        zsum = jnp.zeros_like(top)
        for k in range(PEER_TOPK):
            zsum = zsum + jnp.exp(cv_scr[k] - top)
        rz = 1.0 / zsum
        rk1 = rk1_scr[...]
        n_i = jnp.zeros_like(rk1)
        for r1 in range(PEER_TOPK):
            cnt = jnp.zeros_like(top)
            for p, (a, _) in enumerate(CAND_PAIRS):
                if a == r1:
                    cnt = cnt + jnp.where(crk_scr[p] < topk, 1.0, 0.0)
            n_i = jnp.where(rk1 == float(r1), cnt[None], n_i)
        pk_scr[0] = n_i.reshape(half, LANES)
        pk_scr[1] = (jnp.exp(s1_scr[lt] - v1_scr[0][None]) * rz[None]).reshape(half, LANES)
        pk_scr[2] = rk2_scr[...].reshape(half, LANES)
        pk_scr[3] = jnp.exp(s2_scr[lt] - v2_scr[0][None]).reshape(half, LANES)

    def lane_tile(lt, carry):
        tie = by_value(lt)

        @pl.when(jnp.max(tie) > 0.0)
        def _():
            by_extraction(lt)

        lanes = pl.ds(pl.multiple_of(lt * LANES, LANES), LANES)
        for h in range(H):
            n_ref[h, :, lanes] = pk_scr[0, pl.ds(h, K, stride=H), :]
            e1_ref[h, :, lanes] = pk_scr[1, pl.ds(h, K, stride=H), :]
            for a, o_ref in ((2, r2_ref), (3, e2_ref)):
                lo = pk_scr[a, pl.ds(h, K // 2, stride=H), :]
                hi = pk_scr[a, pl.ds(h + (K // 2) * H, K // 2, stride=H), :]
                packed = _pack_pair(lo, hi)
                for jt in range(K // BF16_ROWS):
                    o_ref[lt, jt, h] = packed[jt * 8:(jt + 1) * 8]
        return carry

    lax.fori_loop(0, nlt, lane_tile, 0)


def _peer_select(h1, g, wq, kb1, kb2, tb):
    T, D = h1.shape
    H, K = PEER_HEADS, PEER_KEYS
    nlt = tb // LANES
    ncand = len(CAND_PAIRS)
    const = lambda shape: pl.BlockSpec(shape, lambda i: (0,) * len(shape))
    hkt = jax.ShapeDtypeStruct((H, K, T), F32)
    hkt_spec = pl.BlockSpec((H, K, tb), lambda i: (0, 0, i))
    pair = jax.ShapeDtypeStruct((T // LANES, K // BF16_ROWS, H, 8, LANES), PACKED)
    pair_spec = pl.BlockSpec((nlt, K // BF16_ROWS, H, 8, LANES), lambda i: (i, 0, 0, 0, 0))
    tile = lambda n: pltpu.VMEM((n, H, LANES), F32)
    return pl.pallas_call(
        functools.partial(_select_kernel, tb=tb),
        grid=(T // tb,),
        in_specs=[pl.BlockSpec((tb, D), lambda i: (i, 0)), const((1, D)), const(wq.shape),
                  const(kb1.shape), const(kb2.shape)],
        out_specs=[pl.BlockSpec((D, tb), lambda i: (0, i)), hkt_spec, hkt_spec, pair_spec, pair_spec],
        out_shape=[jax.ShapeDtypeStruct((D, T), BF16), hkt, hkt, pair, pair],
        scratch_shapes=[pltpu.VMEM((nlt, K, H, LANES), F32), pltpu.VMEM((nlt, K, H, LANES), F32),
                        tile(K), tile(K), tile(K), tile(PEER_TOPK), tile(PEER_TOPK),
                        tile(ncand), tile(ncand), tile(PEER_TOPK),
                        pltpu.VMEM((4, H * K, LANES), F32)],
        compiler_params=_params(("parallel",)),
        name="peer_select",
    )(h1, g, wq, kb1, kb2)


def _gelu_tanh(x):
    c = math.sqrt(2.0 / math.pi)
    hx = 0.5 * x
    return hx * jnp.tanh(x * (c + (c * 0.044715) * (x * x))) + hx


def _dense_kernel(h_ref, xnt_ref, n_ref, e1_ref, r2_ref, e2_ref, u_ref, vt_ref, o_ref,
                  pre0_scr, pre1_scr, act0_scr, act1_scr, acc_scr, *, tb, ec):
    k = pl.program_id(1)
    last = pl.num_programs(1) - 1
    H, K = PEER_HEADS, PEER_KEYS
    ni = ec // K
    zero = jnp.zeros((), BF16)
    lw = 2 * LANES

    def stage_b(cb, pre_ref, act_ref, lp):
        for lt in range(lw // LANES):
            ltile = lp * (lw // LANES) + lt
            lanes = pl.ds(pl.multiple_of(ltile * LANES, LANES), LANES)
            for i8 in range(ni // 8):
                irows = pl.ds(pl.multiple_of(cb * ni + i8 * 8, 8), 8)
                n8 = [n_ref[h, irows, lanes] for h in range(H)]
                e8 = [e1_ref[h, irows, lanes] for h in range(H)]
                for i1 in range(8):
                    il = i8 * 8 + i1
                    nb = [jnp.broadcast_to(n8[h][i1:i1 + 1, :], (BF16_ROWS, LANES)).astype(BF16) for h in range(H)]
                    eb = [jnp.broadcast_to(e8[h][i1:i1 + 1, :], (BF16_ROWS, LANES)).astype(BF16) for h in range(H)]
                    for jt in range(K // BF16_ROWS):
                        gate = None
                        for h in range(H):
                            r = pltpu.bitcast(r2_ref[ltile, jt, h], BF16)
                            e = pltpu.bitcast(e2_ref[ltile, jt, h], BF16)
                            t = jnp.where(r < nb[h], e, zero) * eb[h]
                            gate = t if gate is None else gate + t
                        gate = pltpu.bitcast(gate, PACKED)
                        for half in range(2):
                            g = _unpack_pair(gate, half)
                            r0 = il * K + half * (K // 2) + jt * 8
                            act_ref[r0:r0 + 8, lanes] = (_gelu_tanh(pre_ref[r0:r0 + 8, lanes]) * g).astype(BF16)

    def sub_step(sub, do_a, do_b, do_c):
        s = 2 * k + sub
        first = s == 2
        pre_w, pre_r = (pre0_scr, pre1_scr) if sub == 0 else (pre1_scr, pre0_scr)
        act_r, act_w = (act0_scr, act1_scr) if sub == 0 else (act1_scr, act0_scr)

        def body(lp, carry):
            lanes = pl.ds(pl.multiple_of(lp * lw, lw), lw)
            if do_a:
                u = pltpu.bitcast(u_ref[sub * (ec // 2):(sub + 1) * (ec // 2), :], BF16)
                pre_w[:, lanes] = _mm(u, xnt_ref[:, lanes])
            if do_b:
                stage_b(s - 1, pre_r, act_w, lp)
            if do_c:
                vt = pltpu.bitcast(vt_ref[:, sub * ec:(sub + 1) * ec], BF16)
                contrib = _mm(vt, act_r[:, lanes])
                acc_scr[:, lanes] = jnp.where(first, contrib, acc_scr[:, lanes] + contrib)
            return carry

        lax.fori_loop(0, tb // lw, body, 0)

    @pl.when(k == 0)
    def _():
        sub_step(0, True, False, False)
        sub_step(1, True, True, False)

    @pl.when((k > 0) & (k < last))
    def _():
        sub_step(0, True, True, True)
        sub_step(1, True, True, True)

    @pl.when(k == last)
    def _():
        sub_step(0, False, True, True)
        sub_step(1, False, False, True)

    @pl.when(k == last)
    def _():
        o_ref[...] = h_ref[...] + acc_scr[...].T


def _pack_kernel(w_ref, o_ref, *, transpose):
    w = w_ref[...]
    o_ref[...] = pltpu.bitcast((w.T if transpose else w).astype(BF16), PACKED)


def _pack_table(w, layer, *, transpose, rows=1024):
    _, E, D = w.shape
    if transpose:
        out_shape, out_spec = (D // 2, E), pl.BlockSpec((D // 2, rows), lambda i: (0, i))
    else:
        out_shape, out_spec = (E // 2, D), pl.BlockSpec((rows // 2, D), lambda i: (i, 0))
    return pl.pallas_call(
        functools.partial(_pack_kernel, transpose=transpose),
        grid=(E // rows,),
        in_specs=[pl.BlockSpec((None, rows, D), lambda i: (layer, i, 0))],
        out_specs=out_spec,
        out_shape=jax.ShapeDtypeStruct(out_shape, PACKED),
        compiler_params=_params(("parallel",)),
        name="pack_table_t" if transpose else "pack_table",
    )(w)


def _peer_dense(h1, xnt, n, e1, r2, e2, u, vt, tb, ec):
    T, D = h1.shape
    H, K = PEER_HEADS, PEER_KEYS
    nc = 2 * u.shape[0] // ec
    assert nc % 2 == 0 and nc >= 4
    nsteps = nc // 2 + 1
    hkt_spec = pl.BlockSpec((H, K, tb), lambda i, k: (0, 0, i))
    pair_spec = pl.BlockSpec((tb // LANES, K // BF16_ROWS, H, 8, LANES), lambda i, k: (i, 0, 0, 0, 0))
    return pl.pallas_call(
        functools.partial(_dense_kernel, tb=tb, ec=ec),
        grid=(T // tb, nsteps),
        in_specs=[pl.BlockSpec((tb, D), lambda i, k: (i, 0)),
                  pl.BlockSpec((D, tb), lambda i, k: (0, i)),
                  hkt_spec, hkt_spec, pair_spec, pair_spec,
                  pl.BlockSpec((ec, D), lambda i, k: (jnp.minimum(k, nsteps - 2), 0)),
                  pl.BlockSpec((D // 2, 2 * ec), lambda i, k: (0, jnp.maximum(k - 1, 0)))],
        out_specs=pl.BlockSpec((tb, D), lambda i, k: (i, 0)),
        out_shape=jax.ShapeDtypeStruct((T, D), F32),
        scratch_shapes=[pltpu.VMEM((ec, tb), F32), pltpu.VMEM((ec, tb), F32),
                        pltpu.VMEM((ec, tb), BF16), pltpu.VMEM((ec, tb), BF16), pltpu.VMEM((D, tb), F32)],
        compiler_params=_params(("arbitrary", "arbitrary")),
        name="peer_dense",
    )(h1, xnt, n, e1, r2, e2, u, vt)


def _ple_kernel(h_ref, p_ref, g_ref, wg_ref, wp_ref, gf_ref, *o_refs, split):
    h = h_ref[...]
    gate = jax.nn.sigmoid(_mm(_rms(h, g_ref[...]).astype(BF16), wg_ref[...]))
    h = h + gate * _mm(p_ref[...].astype(BF16), wp_ref[...])
    if split is None:
        o_refs[0][...] = h
        return
    y = _rms(h, gf_ref[...])
    o_refs[1][...] = y

    @pl.when(pl.program_id(0) < split)
    def _():
        o_refs[0][...] = y


def _ple(h, p, g, wg, wp, gf, tm, split_rows=None):
    T, D = h.shape
    P = p.shape[1]
    const = lambda shape: pl.BlockSpec(shape, lambda i: (0,) * len(shape))
    tile = lambda fn: pl.BlockSpec((tm, D), fn)
    if split_rows is None:
        split, out_specs, out_shape = None, tile(lambda i: (i, 0)), jax.ShapeDtypeStruct((T, D), F32)
    else:
        split = split_rows // tm
        out_specs = [tile(lambda i: (jnp.minimum(i, split - 1), 0)), tile(lambda i: (jnp.maximum(i - split, 0), 0))]
        out_shape = [jax.ShapeDtypeStruct((split_rows, D), F32), jax.ShapeDtypeStruct((T - split_rows, D), F32)]
    return pl.pallas_call(
        functools.partial(_ple_kernel, split=split),
        grid=(T // tm,),
        in_specs=[tile(lambda i: (i, 0)), pl.BlockSpec((tm, P), lambda i: (i, 0)),
                  const((1, D)), const((D, D)), const((P, D)), const((1, D))],
        out_specs=out_specs,
        out_shape=out_shape,
        compiler_params=_params(("arbitrary",)),
        name="ple",
    )(h, p, g, wg, wp, gf)


def _tiles(tp, ts):
    tm = next(t for t in (512, 256, 128, 64, 32, 16, 8) if tp % t == 0 and ts % t == 0)
    tb = next(t for t in (512, 256) if (tp + ts) % t == 0)
    return tm, tb


def _pad_lanes(v, fill=0.0):
    return jnp.pad(v.astype(F32), (0, LANES - v.shape[0]), constant_values=fill)[None, :]


def kernel(x_prompt, x_sample, state_ssd, state_ssd_conv, state_cf_conv, p_prompt, p_sample, g_mix, w_in, ssd_conv_w, ssd_conv_b, ssd_dt_bias, ssd_a_log, ssd_d, ssd_norm_g, w_ssd_out, cf_dw_w, cf_dw_b, cf_ln_g, cf_ln_b, w_cf_out, w_o, g_ffn, peer_wq, peer_keys, peer_u, peer_v, g_ple, w_ple_gate, w_ple_proj, g_final):
    depth = w_in.shape[0]
    bp, lp, D = x_prompt.shape
    bs, ls, _ = x_sample.shape
    tp, ts = bp * lp, bs * ls
    T = tp + ts
    inner = SSD_HEADS * SSD_HEAD_DIM
    cd = state_ssd_conv.shape[-1]
    H, K = PEER_HEADS, PEER_KEYS
    tm, tb = _tiles(tp, ts)
    lc_p, lc_s = min(SSD_CHUNK, lp), min(SSD_CHUNK, ls)
    assert lp % lc_p == 0 and ls % lc_s == 0 and tp % lc_s == 0 and D == inner == cf_dw_w.shape[-1]

    h = jnp.concatenate([x_prompt.reshape(tp, D), x_sample.reshape(ts, D)], axis=0)
    expand = (jnp.arange(inner)[None, :] // SSD_HEAD_DIM == jnp.arange(LANES)[:, None]).astype(BF16)
    seqs_per_step = next(n for n in (8, 4, 2, 1) if bs % n == 0 and tp % (n * lc_s) == 0) if ls == lc_s else 1
    zeros_conv = jnp.zeros((1, bp, SSD_CONV - 1, cd), F32)
    zeros_h = jnp.zeros((1, bp, SSD_HEADS, SSD_HEAD_DIM, SSD_STATE), F32)
    zeros_cf = jnp.zeros((1, bp, CF_CONV - 1, D), F32)
    row = lambda v: v.astype(F32)[None, :]
    eye_h = jnp.eye(H, dtype=F32)
    outs = {k: [] for k in ("ssd_p", "sconv_p", "cf_p", "ssd_s", "sconv_s", "cf_s")}

    for i in range(depth):
        w = w_in[i]
        o_xbc, o_dt = inner, inner + cd
        o_glu = o_dt + SSD_HEADS
        w_r = jnp.concatenate([w[:, :o_xbc], w[:, o_xbc:o_dt], w[:, o_glu:],
                               jnp.pad(w[:, o_dt:o_glu], ((0, 0), (0, LANES - SSD_HEADS)))], axis=1).astype(BF16)
        ncols = w_r.shape[1]
        tn = next(t for t in (ncols // 3, ncols) if t % LANES == 0 and ncols % t == 0)
        proj = _inproj(h, row(g_mix[i]), w_r, tm, tn)

        ssd_w = (ssd_conv_w[i], row(ssd_conv_b[i]), _pad_lanes(ssd_dt_bias[i]), _pad_lanes(ssd_a_log[i]),
                 row(jnp.repeat(ssd_d[i], SSD_HEAD_DIM)), row(ssd_norm_g[i]), expand, expand.T)
        ya_p, sconv_p, ssd_p = _ssd(proj, zeros_conv, zeros_h, 0, ssd_w, nb=bp, nc=lp // lc_p, L=lc_p, row_off=0,
                                    nseq=1)
        ya_s, sconv_s, ssd_s = _ssd(proj, state_ssd_conv, state_ssd, i, ssd_w,
                                    nb=bs, nc=ls // lc_s, L=lc_s, row_off=tp, nseq=seqs_per_step)
        cf_w = (cf_dw_w[i], row(cf_dw_b[i]), row(cf_ln_g[i]), row(cf_ln_b[i]))
        yb_p, cf_p = _conformer(proj, zeros_cf, 0, cf_w, nb=bp, nc=lp // lc_p, L=lc_p, row_off=0, nseq=1)
        yb_s, cf_s = _conformer(proj, state_cf_conv, i, cf_w, nb=bs, nc=ls // lc_s, L=lc_s, row_off=tp,
                                nseq=seqs_per_step)
        for k, v in (("ssd_p", ssd_p), ("sconv_p", sconv_p), ("cf_p", cf_p),
                     ("ssd_s", ssd_s), ("sconv_s", sconv_s), ("cf_s", cf_s)):
            outs[k].append(v)
        h = _merge(h, ya_p, ya_s, yb_p, yb_s, proj,
                   w_ssd_out[i].astype(BF16), w_cf_out[i].astype(BF16), w_o[i].astype(BF16), tm)

        qd = peer_keys.shape[-1]
        wq = peer_wq[i].reshape(D, H, 2, qd).transpose(0, 2, 1, 3).reshape(D, 2 * H * qd).astype(BF16)
        kb = [jnp.einsum("hkd,hg->khgd", peer_keys[i, :, s], eye_h).reshape(K * H, H * qd).astype(BF16)
              for s in range(2)]
        xnt, n_sel, e1, r2, e2 = _peer_select(h, row(g_ffn[i]), wq, kb[0], kb[1], tb)
        h = _peer_dense(h, xnt, n_sel, e1, r2, e2, _pack_table(peer_u, i, transpose=False), _pack_table(peer_v, i, transpose=True),
                        tb, ec=8 * K)

        p_all = jnp.concatenate([p_prompt[i].reshape(tp, -1), p_sample[i].reshape(ts, -1)], axis=0)
        h = _ple(h, p_all, row(g_ple[i]), w_ple_gate[i].astype(BF16), w_ple_proj[i].astype(BF16),
                 row(g_final), tm, split_rows=tp if i == depth - 1 else None)

    y_prompt, y_sample = h
    st = {k: jnp.stack(v) for k, v in outs.items()}
    return (y_prompt.reshape(bp, lp, D), y_sample.reshape(bs, ls, D),
            st["ssd_p"], st["sconv_p"], st["cf_p"], st["ssd_s"], st["sconv_s"], st["cf_s"])
```
